```python
import math
import jax, jax.numpy as jnp
from jax import lax
import numpy as np

D_MODEL = 2048
BATCH = 4
SEQ = 2048
DEPTH = 4

MEM_LEN = 256
DN_HEADS = 8
DN_HEAD_DIM = 128
DN_WIDTH = DN_HEADS * DN_HEAD_DIM
DN_CONV = 4
DN_CHUNK = 64
CV_WIDTH = D_MODEL // 2
CV_KERNEL = 31
SB_HEADS = 16
SB_HEAD_DIM = D_MODEL // SB_HEADS
SB_WIDTH = SB_HEADS * SB_HEAD_DIM
SB_BLOCK = 128
MEM_HEADS = 4
MEM_HEAD_DIM = D_MODEL // 8
MEM_WIDTH = MEM_HEADS * MEM_HEAD_DIM

N_EVEN = (DEPTH + 1) // 2
N_ODD = DEPTH // 2
EVEN_IN = 3 * DN_WIDTH + DN_WIDTH + 2 * DN_HEADS + 2 * CV_WIDTH + CV_WIDTH + 2 * MEM_WIDTH
EVEN_MIX = DN_WIDTH + CV_WIDTH + MEM_WIDTH
ODD_IN = 3 * SB_WIDTH + SB_WIDTH + 2 * MEM_WIDTH
ODD_MIX = SB_WIDTH + MEM_WIDTH
EPS = 1e-6

kernel_name = "hybrid_deltanet_conformer_stickbreak_memory"


def rms_norm(x, g):
    xf = x.astype(jnp.float32)
    y = xf * lax.rsqrt(jnp.mean(xf * xf, axis=-1, keepdims=True) + EPS)
    return (y * g.astype(jnp.float32)).astype(x.dtype)


def layer_norm(x, g, b):
    xf = x.astype(jnp.float32)
    mu = jnp.mean(xf, axis=-1, keepdims=True)
    xc = xf - mu
    y = xc * lax.rsqrt(jnp.mean(xc * xc, axis=-1, keepdims=True) + EPS)
    return (y * g.astype(jnp.float32) + b.astype(jnp.float32)).astype(x.dtype)


def l2_norm(x):
    return x * lax.rsqrt(jnp.sum(x * x, axis=-1, keepdims=True) + EPS)


def causal_depthwise_conv(x, w):
    ksz, ch = w.shape
    return lax.conv_general_dilated(
        x, w.astype(x.dtype)[:, None, :], window_strides=(1,), padding=[(ksz - 1, 0)],
        dimension_numbers=("NWC", "WIO", "NWC"), feature_group_count=ch)


def gated_delta_rule(q, k, v, beta, g):
    bsz, seq, nh, dk = q.shape
    dv = v.shape[-1]
    n_chunks = seq // DN_CHUNK

    def chunk(t):
        t = t.reshape((bsz, n_chunks, DN_CHUNK, nh) + t.shape[3:])
        return jnp.moveaxis(t, 3, 1)

    q, k, v, beta, g = chunk(q), chunk(k), chunk(v), chunk(beta), chunk(g)
    gc = jnp.cumsum(g, axis=-1)
    idx = jnp.arange(DN_CHUNK)
    causal = idx[:, None] >= idx[None, :]
    strict = idx[:, None] > idx[None, :]
    decay = jnp.exp(jnp.where(causal, gc[..., :, None] - gc[..., None, :], -jnp.inf))
    kb = k * beta[..., None]
    a_mat = jnp.where(strict, jnp.einsum('bhnid,bhnjd->bhnij', kb, k) * decay, 0.0)
    eye = jnp.eye(DN_CHUNK, dtype=q.dtype)
    rhs = jnp.concatenate([v * beta[..., None], kb * jnp.exp(gc)[..., None]], axis=-1)
    sol = lax.linalg.triangular_solve(eye + a_mat, rhs, left_side=True, lower=True,
                                      unit_diagonal=True)
    u_val, w_key = sol[..., :dv], sol[..., dv:]
    qk = jnp.einsum('bhnid,bhnjd->bhnij', q, k) * decay

    def to_scan(t):
        return jnp.moveaxis(t, 2, 0)

    def step(state, inp):
        q_c, k_c, u_c, w_c, qk_c, gc_c = inp
        v_new = u_c - jnp.einsum('bhcd,bhde->bhce', w_c, state)
        o = (jnp.einsum('bhcd,bhde->bhce', q_c * jnp.exp(gc_c)[..., None], state)
             + jnp.einsum('bhcs,bhse->bhce', qk_c, v_new))
        g_last = gc_c[..., -1:]
        state = (state * jnp.exp(g_last)[..., None]
                 + jnp.einsum('bhcd,bhce->bhde', k_c * jnp.exp(g_last - gc_c)[..., None], v_new))
        return state, o

    state0 = jnp.zeros((bsz, nh, dk, dv), q.dtype)
    _, o = lax.scan(step, state0, (to_scan(q), to_scan(k), to_scan(u_val), to_scan(w_key),
                                   to_scan(qk), to_scan(gc)))
    o = jnp.transpose(o, (1, 0, 3, 2, 4))
    return o.reshape(bsz, seq, nh, dv)


def stick_breaking_attention(q, k, v):
    bsz, seq, nh, hd = q.shape
    scale = hd ** -0.5
    outs = []
    for blk in range(seq // SB_BLOCK):
        t0 = blk * SB_BLOCK
        t1 = t0 + SB_BLOCK
        z = jnp.einsum('bthd,bshd->bhts', q[:, t0:t1], k[:, :t1]) * scale
        t_pos = t0 + jnp.arange(SB_BLOCK)
        s_pos = jnp.arange(t1)
        mask = s_pos[None, :] < t_pos[:, None]
        log_keep = jnp.where(mask, jax.nn.log_sigmoid(-z), 0.0)
        rest = lax.cumsum(log_keep, axis=3, reverse=True) - log_keep
        weights = jnp.where(mask, jnp.exp(jax.nn.log_sigmoid(z) + rest), 0.0)
        outs.append(jnp.einsum('bhts,bshd->bthd', weights, v[:, :t1]))
    return jnp.concatenate(outs, axis=1)


def memory_attend(q_m, z_m, q_norm_g, mem_k, mem_v):
    bsz, seq, _ = q_m.shape
    q = rms_norm(q_m.reshape(bsz, seq, MEM_HEADS, MEM_HEAD_DIM).astype(jnp.float32), q_norm_g)
    s = jnp.einsum('bshd,bmhd->bhsm', q, mem_k.astype(jnp.float32)) * (MEM_HEAD_DIM ** -0.5)
    p = jax.nn.softmax(s, axis=-1)
    o = jnp.einsum('bhsm,bmhd->bshd', p, mem_v.astype(jnp.float32)).reshape(bsz, seq, MEM_WIDTH)
    return (o * jax.nn.silu(z_m.astype(jnp.float32))).astype(q_m.dtype)


def even_layer(h, norm_g, w_in, conv_qkv, a_log, dt_bias, dn_norm_g, dw_w, dw_b,
               ln_g, ln_b, q_norm_m, w_out, mem_k, mem_v):
    bsz, seq, _ = h.shape
    f32 = jnp.float32
    p = rms_norm(h, norm_g) @ w_in
    cuts = [3 * DN_WIDTH, 4 * DN_WIDTH, 4 * DN_WIDTH + DN_HEADS, 4 * DN_WIDTH + 2 * DN_HEADS]
    cuts += [cuts[-1] + 2 * CV_WIDTH, cuts[-1] + 3 * CV_WIDTH, cuts[-1] + 3 * CV_WIDTH + MEM_WIDTH]
    qkv_a, z_a, b_a, a_a, glu_b, z_b, q_m, z_m = jnp.split(p, cuts, axis=-1)

    qkv = jax.nn.silu(causal_depthwise_conv(qkv_a, conv_qkv)).astype(f32)
    q, k, v = [t.reshape(bsz, seq, DN_HEADS, DN_HEAD_DIM) for t in jnp.split(qkv, 3, axis=-1)]
    q = l2_norm(q) * (DN_HEAD_DIM ** -0.5)
    k = l2_norm(k)
    beta = jax.nn.sigmoid(b_a.astype(f32))
    g = -jnp.exp(a_log.astype(f32)) * jax.nn.softplus(a_a.astype(f32) + dt_bias.astype(f32))
    o = gated_delta_rule(q, k, v, beta, g)
    gate_a = jax.nn.silu(z_a.astype(f32)).reshape(bsz, seq, DN_HEADS, DN_HEAD_DIM)
    o_a = (rms_norm(o, dn_norm_g) * gate_a).reshape(bsz, seq, DN_WIDTH).astype(h.dtype)

    glu = glu_b[..., :CV_WIDTH] * jax.nn.sigmoid(glu_b[..., CV_WIDTH:])
    c = causal_depthwise_conv(glu, dw_w) + dw_b.astype(glu.dtype)
    c = layer_norm(c, ln_g, ln_b)
    o_b = (jax.nn.silu(c.astype(f32)) * jax.nn.silu(z_b.astype(f32))).astype(h.dtype)

    o_m = memory_attend(q_m, z_m, q_norm_m, mem_k, mem_v)
    return jnp.concatenate([o_a, o_b, o_m], axis=-1) @ w_out


def odd_layer(h, norm_g, w_in, q_norm_c, k_norm_c, q_norm_m, w_out, mem_k, mem_v):
    bsz, seq, _ = h.shape
    f32 = jnp.float32
    p = rms_norm(h, norm_g) @ w_in
    cuts = [SB_WIDTH, 2 * SB_WIDTH, 3 * SB_WIDTH, 4 * SB_WIDTH, 4 * SB_WIDTH + MEM_WIDTH]
    q_c, k_c, v_c, z_c, q_m, z_m = jnp.split(p, cuts, axis=-1)
    heads = lambda t: t.reshape(bsz, seq, SB_HEADS, SB_HEAD_DIM).astype(f32)
    q = rms_norm(heads(q_c), q_norm_c)
    k = rms_norm(heads(k_c), k_norm_c)
    o = stick_breaking_attention(q, k, heads(v_c)).reshape(bsz, seq, SB_WIDTH)
    o_c = (o * jax.nn.silu(z_c.astype(f32))).astype(h.dtype)
    o_m = memory_attend(q_m, z_m, q_norm_m, mem_k, mem_v)
    return jnp.concatenate([o_c, o_m], axis=-1) @ w_out


def setup_inputs(seed: int = 0) -> dict:
    key = jax.random.key(seed)
    ks = jax.random.split(key, 24)
    f32 = jnp.float32
    nrm = lambda k, shape, scale: jax.random.normal(k, shape, f32) * scale
    gain = lambda k, shape: 1.0 + 0.05 * jax.random.normal(k, shape, f32)
    dt = jnp.exp(jax.random.uniform(ks[9], (N_EVEN, DN_HEADS), f32,
                                    math.log(1e-3), math.log(1e-1)))
    return {
        "x": nrm(ks[0], (BATCH, SEQ, D_MODEL), 1.0),
        "mem": nrm(ks[1], (BATCH, MEM_LEN, D_MODEL), 1.0),
        "mem_norm_g": gain(ks[2], (D_MODEL,)),
        "w_mem_kv": nrm(ks[3], (D_MODEL, 2 * MEM_WIDTH), D_MODEL ** -0.5),
        "mem_k_norm_g": gain(ks[4], (MEM_HEAD_DIM,)),
        "ev_norm_g": gain(ks[5], (N_EVEN, D_MODEL)),
        "ev_w_in": nrm(ks[6], (N_EVEN, D_MODEL, EVEN_IN), D_MODEL ** -0.5),
        "ev_conv_qkv": nrm(ks[7], (N_EVEN, DN_CONV, 3 * DN_WIDTH), DN_CONV ** -0.5),
        "ev_a_log": jnp.log(jax.random.uniform(ks[8], (N_EVEN, DN_HEADS), f32, 1.0, 16.0)),
        "ev_dt_bias": dt + jnp.log(-jnp.expm1(-dt)),
        "ev_dn_norm_g": gain(ks[10], (N_EVEN, DN_HEAD_DIM)),
        "ev_dw_w": nrm(ks[11], (N_EVEN, CV_KERNEL, CV_WIDTH), CV_KERNEL ** -0.5),
        "ev_dw_b": nrm(ks[12], (N_EVEN, CV_WIDTH), 0.02),
        "ev_ln_g": gain(ks[13], (N_EVEN, CV_WIDTH)),
        "ev_ln_b": nrm(ks[14], (N_EVEN, CV_WIDTH), 0.02),
        "ev_q_norm_m": gain(ks[15], (N_EVEN, MEM_HEAD_DIM)),
        "ev_w_out": nrm(ks[16], (N_EVEN, EVEN_MIX, D_MODEL), EVEN_MIX ** -0.5),
        "od_norm_g": gain(ks[17], (N_ODD, D_MODEL)),
        "od_w_in": nrm(ks[18], (N_ODD, D_MODEL, ODD_IN), D_MODEL ** -0.5),
        "od_q_norm_c": gain(ks[19], (N_ODD, SB_HEAD_DIM)),
        "od_k_norm_c": gain(ks[20], (N_ODD, SB_HEAD_DIM)),
        "od_q_norm_m": gain(ks[21], (N_ODD, MEM_HEAD_DIM)),
        "od_w_out": nrm(ks[22], (N_ODD, ODD_MIX, D_MODEL), ODD_MIX ** -0.5),
    }


def reference(x, mem, mem_norm_g, w_mem_kv, mem_k_norm_g,
              ev_norm_g, ev_w_in, ev_conv_qkv, ev_a_log, ev_dt_bias, ev_dn_norm_g,
              ev_dw_w, ev_dw_b, ev_ln_g, ev_ln_b, ev_q_norm_m, ev_w_out,
              od_norm_g, od_w_in, od_q_norm_c, od_k_norm_c, od_q_norm_m, od_w_out):
    bsz, mlen, _ = mem.shape
    mkv = (rms_norm(mem, mem_norm_g) @ w_mem_kv).reshape(bsz, mlen, 2, MEM_HEADS, MEM_HEAD_DIM)
    mem_k = rms_norm(mkv[:, :, 0], mem_k_norm_g)
    mem_v = mkv[:, :, 1]
    h = x
    for layer in range(DEPTH):
        j = layer // 2
        if layer % 2 == 0:
            h = h + even_layer(h, ev_norm_g[j], ev_w_in[j], ev_conv_qkv[j], ev_a_log[j],
                               ev_dt_bias[j], ev_dn_norm_g[j], ev_dw_w[j], ev_dw_b[j],
                               ev_ln_g[j], ev_ln_b[j], ev_q_norm_m[j], ev_w_out[j],
                               mem_k, mem_v)
        else:
            h = h + odd_layer(h, od_norm_g[j], od_w_in[j], od_q_norm_c[j], od_k_norm_c[j],
                              od_q_norm_m[j], od_w_out[j], mem_k, mem_v)
    return h
```

```python
import functools

import jax
import jax.numpy as jnp
from jax import lax
from jax.experimental import pallas as pl
from jax.experimental.pallas import tpu as pltpu

F32 = jnp.float32
BF16 = jnp.bfloat16
EPS = 1e-6

LANES = 128
SUBLANES = 8
VMEM_LIMIT = 48 * 1024 * 1024

DN_HEADS = 8
DN_HD = 128
DN_WIDTH = DN_HEADS * DN_HD
DN_CONV = 4
DN_CHUNK = 128
CV_WIDTH = 1024
CV_KERNEL = 31
CV_HALO = 32
SB_HEADS = 16
SB_HD = 128
SB_WIDTH = SB_HEADS * SB_HD
SB_TQ = 512
SB_TK = 128
MEM_HEADS = 4
MEM_HD = 256
MEM_WIDTH = MEM_HEADS * MEM_HD


def _params(*sem):
    return pltpu.CompilerParams(dimension_semantics=sem, vmem_limit_bytes=VMEM_LIMIT)


def _sigmoid(x):
    return 1.0 / (1.0 + jnp.exp(-x))


def _silu(x):
    return x * _sigmoid(x)


def _softplus(x):
    return jnp.maximum(x, 0.0) + jnp.log(1.0 + jnp.exp(-jnp.abs(x)))


def _mm(a, b):
    return jnp.dot(a.astype(BF16), b.astype(BF16), preferred_element_type=F32)


def _mm3(a, b):
    a_hi = a.astype(BF16)
    b_hi = b.astype(BF16)
    a_lo = (a - a_hi.astype(F32)).astype(BF16)
    b_lo = (b - b_hi.astype(F32)).astype(BF16)
    dot = functools.partial(jnp.dot, preferred_element_type=F32)
    return dot(a_hi, b_hi) + (dot(a_hi, b_lo) + dot(a_lo, b_hi))


def _mm_nt(a, b):
    return lax.dot_general(a.astype(BF16), b.astype(BF16), (((1,), (1,)), ((), ())),
                           preferred_element_type=F32)


def _mm_tn(a, b):
    return lax.dot_general(a.astype(BF16), b.astype(BF16), (((0,), (0,)), ((), ())),
                           preferred_element_type=F32)


def _rmsnorm_kernel(x_ref, g_ref, o_ref):
    x = x_ref[...]
    ms = jnp.mean(x * x, axis=-1, keepdims=True)
    o_ref[...] = (x * lax.rsqrt(ms + EPS) * g_ref[...]).astype(o_ref.dtype)


def _rmsnorm_bf16(x2d, g, tm=512):
    m, d = x2d.shape
    return pl.pallas_call(
        _rmsnorm_kernel,
        grid=(m // tm,),
        in_specs=[pl.BlockSpec((tm, d), lambda i: (i, 0)),
                  pl.BlockSpec((1, d), lambda i: (0, 0))],
        out_specs=pl.BlockSpec((tm, d), lambda i: (i, 0)),
        out_shape=jax.ShapeDtypeStruct((m, d), BF16),
        compiler_params=_params("parallel"),
        name="rmsnorm",
    )(x2d, g.reshape(1, d))


def _proj_kernel(*refs, n_pairs, has_res):
    o_ref = refs[-1]
    acc = None
    for p in range(n_pairs):
        d = jnp.dot(refs[p][...], refs[n_pairs + p][...], preferred_element_type=F32)
        acc = d if acc is None else acc + d
    if has_res:
        acc = refs[2 * n_pairs][...] + acc
    o_ref[...] = acc.astype(o_ref.dtype)


def _proj(xs, ws, res=None, tm=1024, tn=512):
    m = xs[0].shape[0]
    n = ws[0].shape[1]
    tm = min(tm, m)
    tn = min(tn, n)
    in_specs = [pl.BlockSpec((tm, x.shape[1]), lambda i, j: (i, 0)) for x in xs]
    in_specs += [pl.BlockSpec((w.shape[0], tn), lambda i, j: (0, j)) for w in ws]
    args = list(xs) + list(ws)
    if res is not None:
        in_specs.append(pl.BlockSpec((tm, tn), lambda i, j: (i, j)))
        args.append(res)
    return pl.pallas_call(
        functools.partial(_proj_kernel, n_pairs=len(xs), has_res=res is not None),
        grid=(m // tm, n // tn),
        in_specs=in_specs,
        out_specs=pl.BlockSpec((tm, tn), lambda i, j: (i, j)),
        out_shape=jax.ShapeDtypeStruct((m, n), F32),
        compiler_params=_params("parallel", "parallel"),
        name="proj",
    )(*args)


def _gates_kernel(ba_ref, alog_ref, dt_ref, g_ref, gt_ref):
    x = ba_ref[...]
    col = lax.broadcasted_iota(jnp.int32, x.shape, 1)
    row = lax.broadcasted_iota(jnp.int32, x.shape, 0)
    beta = _sigmoid(x)
    g = -jnp.exp(alog_ref[...]) * _softplus(x + dt_ref[...])
    rin = row % DN_CHUNK
    gc = g
    s = 1
    while s < DN_CHUNK:
        gc = gc + jnp.where(rin >= s, pltpu.roll(gc, s, axis=0), 0.0)
        s *= 2
    out = jnp.where(col < DN_HEADS, beta,
                    jnp.where(col < 2 * DN_HEADS, g, pltpu.roll(gc, DN_HEADS, axis=1)))
    g_ref[...] = out
    gt_ref[...] = out.T


def _gates(ba, a_log, dt_bias, ts=512):
    m = ba.shape[0]
    ts = min(ts, m)
    pad = lambda v: jnp.zeros((1, LANES), F32).at[0, DN_HEADS:2 * DN_HEADS].set(v.astype(F32))
    return pl.pallas_call(
        _gates_kernel,
        grid=(m // ts,),
        in_specs=[pl.BlockSpec((ts, LANES), lambda i: (i, 0)),
                  pl.BlockSpec((1, LANES), lambda i: (0, 0)),
                  pl.BlockSpec((1, LANES), lambda i: (0, 0))],
        out_specs=[pl.BlockSpec((ts, LANES), lambda i: (i, 0)),
                   pl.BlockSpec((LANES, ts), lambda i: (0, i))],
        out_shape=[jax.ShapeDtypeStruct((m, LANES), F32),
                   jax.ShapeDtypeStruct((LANES, m), F32)],
        compiler_params=_params("parallel"),
        name="dn_gates",
    )(ba, pad(a_log), pad(dt_bias))


def _dn_kernel(q_ref, k_ref, v_ref, hq_ref, hk_ref, hv_ref, z_ref, g_ref, gt_ref,
               wq_ref, wk_ref, wv_ref, ng_ref, o_ref, state_ref):
    i = pl.program_id(1)
    c = DN_CHUNK

    @pl.when(i == 0)
    def _():
        state_ref[...] = jnp.zeros_like(state_ref)

    def conv_act(cur_ref, halo_ref, w_ref):
        halo = jnp.where(i == 0, 0.0, halo_ref[...])
        x = jnp.concatenate([halo, cur_ref[...]], axis=0)
        w = w_ref[...]
        y = x * w[DN_CONV - 1:DN_CONV, :]
        for tap in range(DN_CONV - 1):
            y = y + pltpu.roll(x, DN_CONV - 1 - tap, axis=0) * w[tap:tap + 1, :]
        return _silu(y[SUBLANES:, :])

    q_all = conv_act(q_ref, hq_ref, wq_ref)
    k_all = conv_act(k_ref, hk_ref, wk_ref)
    v_all = conv_act(v_ref, hv_ref, wv_ref)
    gates = g_ref[...]
    gates_t = gt_ref[...]
    z_all = z_ref[...]
    ng = ng_ref[...]

    ri = lax.broadcasted_iota(jnp.int32, (c, c), 0)
    ci = lax.broadcasted_iota(jnp.int32, (c, c), 1)
    strict = ri > ci
    causal = ri >= ci
    eye = jnp.where(ri == ci, 1.0, 0.0).astype(F32)

    for h in range(DN_HEADS):
        sl = slice(h * DN_HD, (h + 1) * DN_HD)
        qh, kh, vh = q_all[:, sl], k_all[:, sl], v_all[:, sl]
        qh = qh * lax.rsqrt(jnp.sum(qh * qh, axis=-1, keepdims=True) + EPS) * (DN_HD ** -0.5)
        kh = kh * lax.rsqrt(jnp.sum(kh * kh, axis=-1, keepdims=True) + EPS)
        bcol = gates[:, h:h + 1]
        gcol = gates[:, 2 * DN_HEADS + h:2 * DN_HEADS + h + 1]
        grow = gates_t[2 * DN_HEADS + h:2 * DN_HEADS + h + 1, :]
        glast = grow[:, c - 1:c]
        decay = jnp.exp(jnp.minimum(gcol - grow, 0.0))
        kb = kh * bcol
        a_mat = jnp.where(strict, _mm_nt(kb, kh) * decay, 0.0)
        x = -a_mat
        t_inv = eye + x
        p = 2
        while p < c:
            x = _mm3(x, x)
            t_inv = t_inv + _mm3(t_inv, x)
            p *= 2
        egc = jnp.exp(gcol)
        sol = _mm3(t_inv, jnp.concatenate([vh * bcol, kb * egc], axis=-1))
        u_val, w_key = sol[:, :DN_HD], sol[:, DN_HD:]
        qk = jnp.where(causal, _mm_nt(qh, kh) * decay, 0.0)
        state = state_ref[h]
        both = _mm(jnp.concatenate([w_key, qh * egc], axis=0), state)
        v_new = u_val - both[:c]
        o = both[c:] + _mm(qk, v_new)
        kd = kh * jnp.exp(glast - gcol)
        state_ref[h] = state * jnp.exp(glast) + _mm_tn(kd, v_new)
        o = o * lax.rsqrt(jnp.mean(o * o, axis=-1, keepdims=True) + EPS) * ng
        o_ref[:, sl] = (o * _silu(z_all[:, sl])).astype(o_ref.dtype)


def _deltanet(p_main, gates, gates_t, conv_w, norm_g, bsz, seq):
    c = DN_CHUNK
    ns = seq // c
    nw = DN_WIDTH // DN_WIDTH
    del nw
    row = lambda b, i: b * ns + i
    halo_row = lambda b, i: jnp.maximum((b * ns + i) * (c // SUBLANES) - 1, 0)
    cur = lambda cb: pl.BlockSpec((c, DN_WIDTH), lambda b, i: (row(b, i), cb))
    halo = lambda cb: pl.BlockSpec((SUBLANES, DN_WIDTH), lambda b, i: (halo_row(b, i), cb))
    wspec = lambda cb: pl.BlockSpec((DN_CONV, DN_WIDTH), lambda b, i: (0, cb))
    return pl.pallas_call(
        _dn_kernel,
        grid=(bsz, ns),
        in_specs=[cur(0), cur(1), cur(2), halo(0), halo(1), halo(2), cur(3),
                  pl.BlockSpec((c, LANES), lambda b, i: (row(b, i), 0)),
                  pl.BlockSpec((LANES, c), lambda b, i: (0, row(b, i))),
                  wspec(0), wspec(1), wspec(2),
                  pl.BlockSpec((1, DN_HD), lambda b, i: (0, 0))],
        out_specs=pl.BlockSpec((c, DN_WIDTH), lambda b, i: (row(b, i), 0)),
        out_shape=jax.ShapeDtypeStruct((bsz * seq, DN_WIDTH), BF16),
        scratch_shapes=[pltpu.VMEM((DN_HEADS, DN_HD, DN_HD), F32)],
        compiler_params=_params("parallel", "arbitrary"),
        name="deltanet",
    )(p_main, p_main, p_main, p_main, p_main, p_main, p_main, gates, gates_t,
      conv_w, conv_w, conv_w, norm_g.reshape(1, DN_HD))


def _cv_kernel(a_ref, b_ref, ha_ref, hb_ref, z_ref, w_ref, bias_ref, lg_ref, lb_ref, o_ref):
    i = pl.program_id(1)
    ts = a_ref.shape[0]
    glu_cur = a_ref[...] * _sigmoid(b_ref[...])
    glu_halo = jnp.where(i == 0, 0.0, ha_ref[...] * _sigmoid(hb_ref[...]))
    x = jnp.concatenate([glu_halo, glu_cur], axis=0)
    w = w_ref[...]
    shifted = [x] + [pltpu.roll(x, r, axis=0) for r in range(1, SUBLANES)]
    acc = None
    for tap in range(CV_KERNEL):
        back = CV_KERNEL - 1 - tap
        start = CV_HALO - (back // SUBLANES) * SUBLANES
        term = shifted[back % SUBLANES][start:start + ts, :] * w[tap:tap + 1, :]
        acc = term if acc is None else acc + term
    y = acc + bias_ref[...]
    mu = jnp.mean(y, axis=-1, keepdims=True)
    yc = y - mu
    y = yc * lax.rsqrt(jnp.mean(yc * yc, axis=-1, keepdims=True) + EPS) * lg_ref[...] + lb_ref[...]
    o_ref[...] = (_silu(y) * _silu(z_ref[...])).astype(o_ref.dtype)


def _conformer(p_main, cb_a, cb_b, cb_z, dw_w, dw_b, ln_g, ln_b, bsz, seq, ts=256):
    ts = min(ts, seq)
    ns = seq // ts
    row = lambda b, i: b * ns + i
    halo_row = lambda b, i: jnp.maximum((b * ns + i) * (ts // CV_HALO) - 1, 0)
    cur = lambda cb: pl.BlockSpec((ts, CV_WIDTH), lambda b, i: (row(b, i), cb))
    halo = lambda cb: pl.BlockSpec((CV_HALO, CV_WIDTH), lambda b, i: (halo_row(b, i), cb))
    vec = pl.BlockSpec((1, CV_WIDTH), lambda b, i: (0, 0))
    return pl.pallas_call(
        _cv_kernel,
        grid=(bsz, ns),
        in_specs=[cur(cb_a), cur(cb_b), halo(cb_a), halo(cb_b), cur(cb_z),
                  pl.BlockSpec((CV_KERNEL, CV_WIDTH), lambda b, i: (0, 0)), vec, vec, vec],
        out_specs=pl.BlockSpec((ts, CV_WIDTH), lambda b, i: (row(b, i), 0)),
        out_shape=jax.ShapeDtypeStruct((bsz * seq, CV_WIDTH), BF16),
        compiler_params=_params("parallel", "parallel"),
        name="conformer_conv",
    )(p_main, p_main, p_main, p_main, p_main, dw_w,
      dw_b.reshape(1, CV_WIDTH), ln_g.reshape(1, CV_WIDTH), ln_b.reshape(1, CV_WIDTH))


def _mem_kernel(q_ref, z_ref, k_ref, v_ref, qg_ref, kg_ref, o_ref):
    q = q_ref[...]
    qn = q * lax.rsqrt(jnp.mean(q * q, axis=-1, keepdims=True) + EPS) * qg_ref[...]
    k = k_ref[...]
    kn = k * lax.rsqrt(jnp.mean(k * k, axis=-1, keepdims=True) + EPS) * kg_ref[...]
    s = _mm_nt(qn * (MEM_HD ** -0.5), kn)
    p = jnp.exp(s - jnp.max(s, axis=-1, keepdims=True))
    o = _mm(p, v_ref[...]) / jnp.sum(p, axis=-1, keepdims=True)
    o_ref[...] = (o * _silu(z_ref[...])).astype(o_ref.dtype)


def _mem_attend(p, cb_q, cb_z, mkv, q_norm_g, k_norm_g, bsz, seq, ts=512):
    ts = min(ts, seq)
    ns = seq // ts
    mlen = mkv.shape[0] // bsz
    row = lambda b, h, i: b * ns + i
    vec = pl.BlockSpec((1, MEM_HD), lambda b, h, i: (0, 0))
    return pl.pallas_call(
        _mem_kernel,
        grid=(bsz, MEM_HEADS, ns),
        in_specs=[pl.BlockSpec((ts, MEM_HD), lambda b, h, i: (row(b, h, i), cb_q + h)),
                  pl.BlockSpec((ts, MEM_HD), lambda b, h, i: (row(b, h, i), cb_z + h)),
                  pl.BlockSpec((mlen, MEM_HD), lambda b, h, i: (b, h)),
                  pl.BlockSpec((mlen, MEM_HD), lambda b, h, i: (b, MEM_HEADS + h)),
                  vec, vec],
        out_specs=pl.BlockSpec((ts, MEM_HD), lambda b, h, i: (row(b, h, i), h)),
        out_shape=jax.ShapeDtypeStruct((bsz * seq, MEM_WIDTH), BF16),
        compiler_params=_params("parallel", "parallel", "parallel"),
        name="mem_attend",
    )(p, p, mkv, mkv, q_norm_g.reshape(1, MEM_HD), k_norm_g.reshape(1, MEM_HD))


def _sb_kernel(q_ref, k_ref, v_ref, z_ref, qg_ref, kg_ref, o_ref, kn_ref, vb_ref, acc_ref, r_ref):
    i = pl.program_id(2)
    tq, tk = SB_TQ, SB_TK
    tq = q_ref.shape[0]

    @pl.when(i == 0)
    def _():
        k = k_ref[...]
        kn = k * lax.rsqrt(jnp.mean(k * k, axis=-1, keepdims=True) + EPS) * kg_ref[...]
        kn_ref[...] = kn.astype(BF16)
        vb_ref[...] = v_ref[...].astype(BF16)

    q = q_ref[...]
    qn = q * lax.rsqrt(jnp.mean(q * q, axis=-1, keepdims=True) + EPS) * qg_ref[...]
    qn = (qn * (SB_HD ** -0.5)).astype(BF16)
    acc_ref[...] = jnp.zeros_like(acc_ref)
    r_ref[...] = jnp.zeros_like(r_ref)

    uj = lax.broadcasted_iota(jnp.int32, (tk, tk + LANES), 0)
    us = lax.broadcasted_iota(jnp.int32, (tk, tk + LANES), 1)
    u_aug = jnp.where((uj >= us) | (us >= tk), 1.0, 0.0).astype(BF16)

    def block(key_start, vis):
        kb = kn_ref[pl.ds(key_start, tk), :]
        vb = vb_ref[pl.ds(key_start, tk), :]
        z = lax.dot_general(qn, kb, (((1,), (1,)), ((), ())), preferred_element_type=F32)
        lk = jnp.minimum(-z, 0.0) - jnp.log(1.0 + jnp.exp(-jnp.abs(z)))
        if vis is not None:
            lk = jnp.where(vis, lk, 0.0)
        hi = lk.astype(BF16)
        lo = (lk - hi.astype(F32)).astype(BF16)
        inc = (jnp.dot(hi, u_aug, preferred_element_type=F32)
               + jnp.dot(lo, u_aug, preferred_element_type=F32))
        run = r_ref[...]
        w = jnp.exp(z + inc[:, :tk] + run)
        if vis is not None:
            w = jnp.where(vis, w, 0.0)
        acc_ref[...] += jnp.dot(w.astype(BF16), vb, preferred_element_type=F32)
        r_ref[...] = run + inc[:, tk:]

    tpos = lax.broadcasted_iota(jnp.int32, (tq, tk), 0)
    spos = lax.broadcasted_iota(jnp.int32, (tq, tk), 1)
    q_start = pl.multiple_of(i * tq, tq)
    for jl in range(tq // tk - 1, -1, -1):
        block(q_start + jl * tk, (spos + jl * tk) < tpos)

    n_full = i * (tq // tk)

    def body(jj, carry):
        j = n_full - 1 - jj
        block(pl.multiple_of(j * tk, tk), None)
        return carry

    lax.fori_loop(0, n_full, body, 0)
    o_ref[...] = (acc_ref[...] * _silu(z_ref[...])).astype(o_ref.dtype)


def _stickbreak(p, q_norm_g, k_norm_g, bsz, seq):
    tq = min(SB_TQ, seq)
    nq = seq // tq
    nh = SB_HEADS
    vec = pl.BlockSpec((1, SB_HD), lambda b, h, i: (0, 0))
    return pl.pallas_call(
        _sb_kernel,
        grid=(bsz, nh, nq),
        in_specs=[pl.BlockSpec((tq, SB_HD), lambda b, h, i: (b * nq + i, h)),
                  pl.BlockSpec((seq, SB_HD), lambda b, h, i: (b, nh + h)),
                  pl.BlockSpec((seq, SB_HD), lambda b, h, i: (b, 2 * nh + h)),
                  pl.BlockSpec((tq, SB_HD), lambda b, h, i: (b * nq + i, 3 * nh + h)),
                  vec, vec],
        out_specs=pl.BlockSpec((tq, SB_HD), lambda b, h, i: (b * nq + i, h)),
        out_shape=jax.ShapeDtypeStruct((bsz * seq, SB_WIDTH), BF16),
        scratch_shapes=[pltpu.VMEM((seq, SB_HD), BF16), pltpu.VMEM((seq, SB_HD), BF16),
                        pltpu.VMEM((tq, SB_HD), F32), pltpu.VMEM((tq, LANES), F32)],
        compiler_params=_params("parallel", "parallel", "arbitrary"),
        name="stickbreak",
    )(p, p, p, p, q_norm_g.reshape(1, SB_HD), k_norm_g.reshape(1, SB_HD))


def _even_layer(h, mkv, k_norm_g, norm_g, w_in, conv_qkv, a_log, dt_bias, dn_norm_g, dw_w, dw_b,
                ln_g, ln_b, q_norm_m, w_out, bsz, seq):
    c_ba = 4 * DN_WIDTH
    w_main = jnp.concatenate([w_in[:, :c_ba], w_in[:, c_ba + 2 * DN_HEADS:]], axis=1).astype(BF16)
    w_ba = jnp.pad(w_in[:, c_ba:c_ba + 2 * DN_HEADS], ((0, 0), (0, LANES - 2 * DN_HEADS))).astype(BF16)
    xn = _rmsnorm_bf16(h, norm_g)
    p_main = _proj([xn], [w_main])
    ba = _proj([xn], [w_ba])
    gates, gates_t = _gates(ba, a_log, dt_bias)
    o_a = _deltanet(p_main, gates, gates_t, conv_qkv, dn_norm_g, bsz, seq)
    o_b = _conformer(p_main, 4, 5, 6, dw_w, dw_b, ln_g, ln_b, bsz, seq)
    o_m = _mem_attend(p_main, 7 * MEM_HEADS, 8 * MEM_HEADS, mkv, q_norm_m, k_norm_g, bsz, seq)
    wo = w_out.astype(BF16)
    return _proj([o_a, o_b, o_m],
                 [wo[:DN_WIDTH], wo[DN_WIDTH:DN_WIDTH + CV_WIDTH], wo[DN_WIDTH + CV_WIDTH:]], res=h)


def _odd_layer(h, mkv, k_norm_g, norm_g, w_in, q_norm_c, k_norm_c, q_norm_m, w_out, bsz, seq):
    xn = _rmsnorm_bf16(h, norm_g)
    p = _proj([xn], [w_in.astype(BF16)])
    o_c = _stickbreak(p, q_norm_c, k_norm_c, bsz, seq)
    o_m = _mem_attend(p, 8 * MEM_HEADS, 9 * MEM_HEADS, mkv, q_norm_m, k_norm_g, bsz, seq)
    wo = w_out.astype(BF16)
    return _proj([o_c, o_m], [wo[:SB_WIDTH], wo[SB_WIDTH:]], res=h)


def kernel(x, mem, mem_norm_g, w_mem_kv, mem_k_norm_g, ev_norm_g, ev_w_in, ev_conv_qkv, ev_a_log,
           ev_dt_bias, ev_dn_norm_g, ev_dw_w, ev_dw_b, ev_ln_g, ev_ln_b, ev_q_norm_m, ev_w_out,
           od_norm_g, od_w_in, od_q_norm_c, od_k_norm_c, od_q_norm_m, od_w_out):
    bsz, seq, d = x.shape
    mlen = mem.shape[1]
    depth = ev_norm_g.shape[0] + od_norm_g.shape[0]
    mkv = _proj([_rmsnorm_bf16(mem.reshape(bsz * mlen, d), mem_norm_g, tm=256)],
                [w_mem_kv.astype(BF16)], tm=bsz * mlen)
    h = x.reshape(bsz * seq, d)
    for layer in range(depth):
        j = layer // 2
        if layer % 2 == 0:
            h = _even_layer(h, mkv, mem_k_norm_g, ev_norm_g[j], ev_w_in[j], ev_conv_qkv[j],
                            ev_a_log[j], ev_dt_bias[j], ev_dn_norm_g[j], ev_dw_w[j], ev_dw_b[j],
                            ev_ln_g[j], ev_ln_b[j], ev_q_norm_m[j], ev_w_out[j], bsz, seq)
        else:
            h = _odd_layer(h, mkv, mem_k_norm_g, od_norm_g[j], od_w_in[j], od_q_norm_c[j],
                           od_k_norm_c[j], od_q_norm_m[j], od_w_out[j], bsz, seq)
    return h.reshape(bsz, seq, d)
```

```python
import functools

import jax
import jax.numpy as jnp
from jax import lax
from jax.experimental import pallas as pl
from jax.experimental.pallas import tpu as pltpu

F32 = jnp.float32
BF16 = jnp.bfloat16
EPS = 1e-6
LOG2E = 1.4426950408889634

LANES = 128
SUBLANES = 8
VMEM_LIMIT = 48 * 1024 * 1024

DN_HEADS = 8
DN_HD = 128
DN_WIDTH = DN_HEADS * DN_HD
DN_CONV = 4
DN_CHUNK = 128
DN_SPLIT_MAX_POWER = 8
CV_WIDTH = 1024
CV_KERNEL = 31
CV_HALO = 32
SB_HEADS = 16
SB_HD = 128
SB_WIDTH = SB_HEADS * SB_HD
SB_TQ = 512
SB_TK = 128
MEM_HEADS = 4
MEM_HD = 256
MEM_WIDTH = MEM_HEADS * MEM_HD


def _params(*sem):
    return pltpu.CompilerParams(dimension_semantics=sem, vmem_limit_bytes=VMEM_LIMIT)


def _sigmoid(x):
    return 1.0 / (1.0 + jnp.exp(-x))


def _silu(x):
    return x * _sigmoid(x)


def _softplus(x):
    return jnp.maximum(x, 0.0) + jnp.log(1.0 + jnp.exp(-jnp.abs(x)))


def _mm(a, b):
    return jnp.dot(a.astype(BF16), b.astype(BF16), preferred_element_type=F32)


def _mm_nt(a, b):
    return lax.dot_general(a.astype(BF16), b.astype(BF16), (((1,), (1,)), ((), ())),
                           preferred_element_type=F32)


def _bmm(a, b):
    return lax.dot_general(a.astype(BF16), b.astype(BF16), (((2,), (1,)), ((0,), (0,))),
                           preferred_element_type=F32)


def _bmm3(a, b):
    a_hi = a.astype(BF16)
    b_hi = b.astype(BF16)
    a_lo = (a - a_hi.astype(F32)).astype(BF16)
    b_lo = (b - b_hi.astype(F32)).astype(BF16)
    lhs = jnp.concatenate([a_hi, a_hi, a_lo], axis=2)
    rhs = jnp.concatenate([b_hi, b_lo, b_hi], axis=1)
    return lax.dot_general(lhs, rhs, (((2,), (1,)), ((0,), (0,))), preferred_element_type=F32)


def _bmm_nt(a, b):
    return lax.dot_general(a.astype(BF16), b.astype(BF16), (((2,), (2,)), ((0,), (0,))),
                           preferred_element_type=F32)


def _rmsnorm_kernel(x_ref, g_ref, o_ref):
    x = x_ref[...]
    ms = jnp.mean(x * x, axis=-1, keepdims=True)
    o_ref[...] = (x * lax.rsqrt(ms + EPS) * g_ref[...]).astype(o_ref.dtype)


def _rmsnorm_bf16(x2d, g, tm=512):
    m, d = x2d.shape
    return pl.pallas_call(
        _rmsnorm_kernel,
        grid=(m // tm,),
        in_specs=[pl.BlockSpec((tm, d), lambda i: (i, 0)),
                  pl.BlockSpec((1, d), lambda i: (0, 0))],
        out_specs=pl.BlockSpec((tm, d), lambda i: (i, 0)),
        out_shape=jax.ShapeDtypeStruct((m, d), BF16),
        compiler_params=_params("parallel"),
        name="rmsnorm",
    )(x2d, g.reshape(1, d))


def _proj_kernel(*refs, n_pairs, has_res):
    o_ref = refs[-1]
    acc = None
    for p in range(n_pairs):
        d = jnp.dot(refs[p][...], refs[n_pairs + p][...], preferred_element_type=F32)
        acc = d if acc is None else acc + d
    if has_res:
        acc = refs[2 * n_pairs][...] + acc
    o_ref[...] = acc.astype(o_ref.dtype)


def _proj(xs, ws, res=None, tm=1024, tn=512):
    m = xs[0].shape[0]
    n = ws[0].shape[1]
    tm = min(tm, m)
    tn = min(tn, n)
    in_specs = [pl.BlockSpec((tm, x.shape[1]), lambda i, j: (i, 0)) for x in xs]
    in_specs += [pl.BlockSpec((w.shape[0], tn), lambda i, j: (0, j)) for w in ws]
    args = list(xs) + list(ws)
    if res is not None:
        in_specs.append(pl.BlockSpec((tm, tn), lambda i, j: (i, j)))
        args.append(res)
    return pl.pallas_call(
        functools.partial(_proj_kernel, n_pairs=len(xs), has_res=res is not None),
        grid=(m // tm, n // tn),
        in_specs=in_specs,
        out_specs=pl.BlockSpec((tm, tn), lambda i, j: (i, j)),
        out_shape=jax.ShapeDtypeStruct((m, n), F32),
        compiler_params=_params("parallel", "parallel"),
        name="proj",
    )(*args)


def _gates_kernel(ba_ref, alog_ref, dt_ref, g_ref, gt_ref):
    x = ba_ref[...]
    col = lax.broadcasted_iota(jnp.int32, x.shape, 1)
    row = lax.broadcasted_iota(jnp.int32, x.shape, 0)
    beta = _sigmoid(x)
    g = -jnp.exp(alog_ref[...]) * _softplus(x + dt_ref[...])
    rin = row & (DN_CHUNK - 1)
    gc = g
    s = 1
    while s < DN_CHUNK:
        gc = gc + jnp.where(rin >= s, pltpu.roll(gc, s, axis=0), 0.0)
        s *= 2
    out = jnp.where(col < DN_HEADS, beta,
                    jnp.where(col < 2 * DN_HEADS, g, pltpu.roll(gc, DN_HEADS, axis=1)))
    g_ref[...] = out
    gt_ref[...] = out.T


def _gates(ba, a_log, dt_bias, ts=512):
    m = ba.shape[0]
    ts = min(ts, m)
    pad = lambda v: jnp.zeros((1, LANES), F32).at[0, DN_HEADS:2 * DN_HEADS].set(v.astype(F32))
    return pl.pallas_call(
        _gates_kernel,
        grid=(m // ts,),
        in_specs=[pl.BlockSpec((ts, LANES), lambda i: (i, 0)),
                  pl.BlockSpec((1, LANES), lambda i: (0, 0)),
                  pl.BlockSpec((1, LANES), lambda i: (0, 0))],
        out_specs=[pl.BlockSpec((ts, LANES), lambda i: (i, 0)),
                   pl.BlockSpec((LANES, ts), lambda i: (0, i))],
        out_shape=[jax.ShapeDtypeStruct((m, LANES), F32),
                   jax.ShapeDtypeStruct((LANES, m), F32)],
        compiler_params=_params("parallel"),
        name="dn_gates",
    )(ba, pad(a_log), pad(dt_bias))


def _dn_kernel(q_ref, k_ref, v_ref, hq_ref, hk_ref, hv_ref, z_ref, g_ref, gt_ref,
               wq_ref, wk_ref, wv_ref, ng_ref, o_ref, state_ref):
    i = pl.program_id(1)
    c = DN_CHUNK

    @pl.when(i == 0)
    def _():
        state_ref[...] = jnp.zeros_like(state_ref)

    def conv_act(cur_ref, halo_ref, w_ref):
        halo = jnp.where(i == 0, 0.0, halo_ref[...])
        x = jnp.concatenate([halo, cur_ref[...]], axis=0)
        w = w_ref[...]
        y = x * w[DN_CONV - 1:DN_CONV, :]
        for tap in range(DN_CONV - 1):
            y = y + pltpu.roll(x, DN_CONV - 1 - tap, axis=0) * w[tap:tap + 1, :]
        return _silu(y[SUBLANES:, :])

    heads = lambda t: jnp.stack([t[:, h * DN_HD:(h + 1) * DN_HD] for h in range(DN_HEADS)])
    q = heads(conv_act(q_ref, hq_ref, wq_ref))
    k = heads(conv_act(k_ref, hk_ref, wk_ref))
    v = heads(conv_act(v_ref, hv_ref, wv_ref))
    q = q * lax.rsqrt(jnp.sum(q * q, axis=-1, keepdims=True) + EPS) * (DN_HD ** -0.5)
    k = k * lax.rsqrt(jnp.sum(k * k, axis=-1, keepdims=True) + EPS)
    gates = g_ref[...]
    gates_t = gt_ref[...]
    beta = jnp.stack([gates[:, h:h + 1] for h in range(DN_HEADS)])
    gcol = jnp.stack([gates[:, 2 * DN_HEADS + h:2 * DN_HEADS + h + 1] for h in range(DN_HEADS)])
    grow = jnp.stack([gates_t[2 * DN_HEADS + h:2 * DN_HEADS + h + 1, :] for h in range(DN_HEADS)])
    glast = grow[:, :, c - 1:c]

    ri = lax.broadcasted_iota(jnp.int32, (DN_HEADS, c, c), 1)
    ci = lax.broadcasted_iota(jnp.int32, (DN_HEADS, c, c), 2)
    decay = jnp.exp(jnp.minimum(gcol - grow, 0.0))
    kb = k * beta
    s2 = _bmm_nt(jnp.concatenate([kb, q], axis=1), k)
    a_mat = jnp.where(ri > ci, s2[:, :c] * decay, 0.0)
    qk = jnp.where(ri >= ci, s2[:, c:] * decay, 0.0)
    x = _bmm3(a_mat, a_mat)
    n = -a_mat
    p = 2
    while 2 * p < c:
        mm = _bmm3 if p <= DN_SPLIT_MAX_POWER else _bmm
        both = mm(x, jnp.concatenate([x, n], axis=-1))
        n = n + x + both[:, :, c:]
        x = both[:, :, :c]
        p *= 2
    n = n + x + _bmm(x, n)
    egc = jnp.exp(gcol)
    rhs = jnp.concatenate([v * beta, kb * egc], axis=-1)
    sol = rhs + _bmm(n, rhs)
    u_val, w_key = sol[:, :, :DN_HD], sol[:, :, DN_HD:]
    state = state_ref[...]
    both = _bmm(jnp.concatenate([w_key, q * egc], axis=1), state)
    v_new = u_val - both[:, :c]
    kd_t = jnp.swapaxes(k * jnp.exp(glast - gcol), 1, 2)
    tail = _bmm(jnp.concatenate([qk, kd_t], axis=1), v_new)
    o = both[:, c:] + tail[:, :c]
    state_ref[...] = state * jnp.exp(glast) + tail[:, c:]
    o = o * lax.rsqrt(jnp.mean(o * o, axis=-1, keepdims=True) + EPS) * ng_ref[...]
    z_all = z_ref[...]
    for h in range(DN_HEADS):
        sl = slice(h * DN_HD, (h + 1) * DN_HD)
        o_ref[:, sl] = (o[h] * _silu(z_all[:, sl])).astype(o_ref.dtype)


def _deltanet(p_main, gates, gates_t, conv_w, norm_g, bsz, seq):
    c = DN_CHUNK
    ns = seq // c
    row = lambda b, i: b * ns + i
    halo_row = lambda b, i: jnp.maximum((b * ns + i) * (c // SUBLANES) - 1, 0)
    cur = lambda cb: pl.BlockSpec((c, DN_WIDTH), lambda b, i: (row(b, i), cb))
    halo = lambda cb: pl.BlockSpec((SUBLANES, DN_WIDTH), lambda b, i: (halo_row(b, i), cb))
    wspec = lambda cb: pl.BlockSpec((DN_CONV, DN_WIDTH), lambda b, i: (0, cb))
    return pl.pallas_call(
        _dn_kernel,
        grid=(bsz, ns),
        in_specs=[cur(0), cur(1), cur(2), halo(0), halo(1), halo(2), cur(3),
                  pl.BlockSpec((c, LANES), lambda b, i: (row(b, i), 0)),
                  pl.BlockSpec((LANES, c), lambda b, i: (0, row(b, i))),
                  wspec(0), wspec(1), wspec(2),
                  pl.BlockSpec((1, DN_HD), lambda b, i: (0, 0))],
        out_specs=pl.BlockSpec((c, DN_WIDTH), lambda b, i: (row(b, i), 0)),
        out_shape=jax.ShapeDtypeStruct((bsz * seq, DN_WIDTH), BF16),
        scratch_shapes=[pltpu.VMEM((DN_HEADS, DN_HD, DN_HD), F32)],
        compiler_params=_params("parallel", "arbitrary"),
        name="deltanet",
    )(p_main, p_main, p_main, p_main, p_main, p_main, p_main, gates, gates_t,
      conv_w, conv_w, conv_w, norm_g.reshape(1, DN_HD))


def _cv_kernel(a_ref, b_ref, ha_ref, hb_ref, z_ref, w_ref, bias_ref, lg_ref, lb_ref, o_ref):
    i = pl.program_id(1)
    ts = a_ref.shape[0]
    glu_cur = a_ref[...] * _sigmoid(b_ref[...])
    glu_halo = jnp.where(i == 0, 0.0, ha_ref[...] * _sigmoid(hb_ref[...]))
    x = jnp.concatenate([glu_halo, glu_cur], axis=0)
    w = w_ref[...]
    shifted = [x] + [pltpu.roll(x, r, axis=0) for r in range(1, SUBLANES)]
    acc = None
    for tap in range(CV_KERNEL):
        back = CV_KERNEL - 1 - tap
        start = CV_HALO - (back // SUBLANES) * SUBLANES
        term = shifted[back % SUBLANES][start:start + ts, :] * w[tap:tap + 1, :]
        acc = term if acc is None else acc + term
    y = acc + bias_ref[...]
    mu = jnp.mean(y, axis=-1, keepdims=True)
    yc = y - mu
    y = yc * lax.rsqrt(jnp.mean(yc * yc, axis=-1, keepdims=True) + EPS) * lg_ref[...] + lb_ref[...]
    o_ref[...] = (_silu(y) * _silu(z_ref[...])).astype(o_ref.dtype)


def _conformer(p_main, cb_a, cb_b, cb_z, dw_w, dw_b, ln_g, ln_b, bsz, seq, ts=256):
    ts = min(ts, seq)
    ns = seq // ts
    row = lambda b, i: b * ns + i
    halo_row = lambda b, i: jnp.maximum((b * ns + i) * (ts // CV_HALO) - 1, 0)
    cur = lambda cb: pl.BlockSpec((ts, CV_WIDTH), lambda b, i: (row(b, i), cb))
    halo = lambda cb: pl.BlockSpec((CV_HALO, CV_WIDTH), lambda b, i: (halo_row(b, i), cb))
    vec = pl.BlockSpec((1, CV_WIDTH), lambda b, i: (0, 0))
    return pl.pallas_call(
        _cv_kernel,
        grid=(bsz, ns),
        in_specs=[cur(cb_a), cur(cb_b), halo(cb_a), halo(cb_b), cur(cb_z),
                  pl.BlockSpec((CV_KERNEL, CV_WIDTH), lambda b, i: (0, 0)), vec, vec, vec],
        out_specs=pl.BlockSpec((ts, CV_WIDTH), lambda b, i: (row(b, i), 0)),
        out_shape=jax.ShapeDtypeStruct((bsz * seq, CV_WIDTH), BF16),
        compiler_params=_params("parallel", "parallel"),
        name="conformer_conv",
    )(p_main, p_main, p_main, p_main, p_main, dw_w,
      dw_b.reshape(1, CV_WIDTH), ln_g.reshape(1, CV_WIDTH), ln_b.reshape(1, CV_WIDTH))


def _mem_kernel(q_ref, z_ref, k_ref, v_ref, qg_ref, kg_ref, o_ref):
    q = q_ref[...]
    qn = q * lax.rsqrt(jnp.mean(q * q, axis=-1, keepdims=True) + EPS) * qg_ref[...]
    k = k_ref[...]
    kn = k * lax.rsqrt(jnp.mean(k * k, axis=-1, keepdims=True) + EPS) * kg_ref[...]
    s = _mm_nt(qn * (MEM_HD ** -0.5), kn)
    p = jnp.exp(s - jnp.max(s, axis=-1, keepdims=True))
    o = _mm(p, v_ref[...]) / jnp.sum(p, axis=-1, keepdims=True)
    o_ref[...] = (o * _silu(z_ref[...])).astype(o_ref.dtype)


def _mem_attend(p, cb_q, cb_z, mkv, q_norm_g, k_norm_g, bsz, seq, ts=512):
    ts = min(ts, seq)
    ns = seq // ts
    mlen = mkv.shape[0] // bsz
    row = lambda b, h, i: b * ns + i
    vec = pl.BlockSpec((1, MEM_HD), lambda b, h, i: (0, 0))
    return pl.pallas_call(
        _mem_kernel,
        grid=(bsz, MEM_HEADS, ns),
        in_specs=[pl.BlockSpec((ts, MEM_HD), lambda b, h, i: (row(b, h, i), cb_q + h)),
                  pl.BlockSpec((ts, MEM_HD), lambda b, h, i: (row(b, h, i), cb_z + h)),
                  pl.BlockSpec((mlen, MEM_HD), lambda b, h, i: (b, h)),
                  pl.BlockSpec((mlen, MEM_HD), lambda b, h, i: (b, MEM_HEADS + h)),
                  vec, vec],
        out_specs=pl.BlockSpec((ts, MEM_HD), lambda b, h, i: (row(b, h, i), h)),
        out_shape=jax.ShapeDtypeStruct((bsz * seq, MEM_WIDTH), BF16),
        compiler_params=_params("parallel", "parallel", "parallel"),
        name="mem_attend",
    )(p, p, mkv, mkv, q_norm_g.reshape(1, MEM_HD), k_norm_g.reshape(1, MEM_HD))


def _sb_kernel(q_ref, k_ref, v_ref, z_ref, qg_ref, kg_ref, o_ref, kn_ref, vb_ref, acc_ref, r_ref):
    i = pl.program_id(2)
    tq, tk = SB_TQ, SB_TK
    tq = q_ref.shape[0]

    @pl.when(i == 0)
    def _():
        k = k_ref[...]
        kn = k * lax.rsqrt(jnp.mean(k * k, axis=-1, keepdims=True) + EPS) * kg_ref[...]
        kn_ref[...] = kn.astype(BF16)
        vb_ref[...] = v_ref[...].astype(BF16)

    q = q_ref[...]
    qn = q * lax.rsqrt(jnp.mean(q * q, axis=-1, keepdims=True) + EPS) * qg_ref[...]
    qn = (qn * (SB_HD ** -0.5 * LOG2E)).astype(BF16)
    nsub = tq // tk

    uj = lax.broadcasted_iota(jnp.int32, (2 * tk, tk + LANES), 0) & (tk - 1)
    us = lax.broadcasted_iota(jnp.int32, (2 * tk, tk + LANES), 1)
    u_aug = jnp.where((uj >= us) | (us >= tk), 1.0, 0.0).astype(BF16)

    sign_bit = jnp.uint32(0x80000000)

    def sub_block(z, vis):
        neg_abs = lax.bitcast_convert_type(lax.bitcast_convert_type(z, jnp.uint32) | sign_bit, F32)
        cost = jnp.maximum(z, 0.0) + jnp.log2(1.0 + jnp.exp2(neg_abs))
        if vis is not None:
            cost = jnp.where(vis, cost, 0.0)
        hi = cost.astype(BF16)
        lo = (cost - hi.astype(F32)).astype(BF16)
        inc = jnp.dot(jnp.concatenate([hi, lo], axis=1), u_aug, preferred_element_type=F32)
        return z - inc[:, :tk], inc[:, tk:]

    q_start = pl.multiple_of(i * tq, tq)
    acc = jnp.zeros((tq, SB_HD), F32)
    run = jnp.zeros((tq, LANES), F32)
    for s in range(nsub - 1, -1, -1):
        r0 = s * tk
        rows = tq - r0
        kb = kn_ref[pl.ds(pl.multiple_of(q_start + r0, tk), tk), :]
        vb = vb_ref[pl.ds(pl.multiple_of(q_start + r0, tk), tk), :]
        z = lax.dot_general(qn[r0:], kb, (((1,), (1,)), ((), ())), preferred_element_type=F32)
        vis = (lax.broadcasted_iota(jnp.int32, (rows, tk), 1)
               < lax.broadcasted_iota(jnp.int32, (rows, tk), 0))
        pre, tot = sub_block(z, vis)
        w = jnp.where(vis, jnp.exp2(pre + run[r0:]), 0.0).astype(BF16)
        pv = jnp.dot(w, vb, preferred_element_type=F32)
        if r0:
            acc = jnp.concatenate([acc[:r0], acc[r0:] + pv], axis=0)
            run = jnp.concatenate([run[:r0], run[r0:] - tot], axis=0)
        else:
            acc = acc + pv
            run = run - tot
    acc_ref[...] = acc
    r_ref[...] = run

    def body(jj, carry):
        key_start = pl.multiple_of((i - 1 - jj) * tq, tq)
        kb = kn_ref[pl.ds(key_start, tq), :]
        vb = vb_ref[pl.ds(key_start, tq), :]
        z_all = lax.dot_general(qn, kb, (((1,), (1,)), ((), ())), preferred_element_type=F32)
        run = r_ref[...]
        ws = [None] * nsub
        for s in range(nsub - 1, -1, -1):
            pre, tot = sub_block(z_all[:, s * tk:(s + 1) * tk], None)
            ws[s] = jnp.exp2(pre + run).astype(BF16)
            run = run - tot
        acc_ref[...] += jnp.dot(jnp.concatenate(ws, axis=1), vb, preferred_element_type=F32)
        r_ref[...] = run
        return carry

    lax.fori_loop(0, i, body, 0)
    o_ref[...] = (acc_ref[...] * _silu(z_ref[...])).astype(o_ref.dtype)


def _stickbreak(p, q_norm_g, k_norm_g, bsz, seq):
    tq = min(SB_TQ, seq)
    nq = seq // tq
    nh = SB_HEADS
    vec = pl.BlockSpec((1, SB_HD), lambda b, h, i: (0, 0))
    return pl.pallas_call(
        _sb_kernel,
        grid=(bsz, nh, nq),
        in_specs=[pl.BlockSpec((tq, SB_HD), lambda b, h, i: (b * nq + i, h)),
                  pl.BlockSpec((seq, SB_HD), lambda b, h, i: (b, nh + h)),
                  pl.BlockSpec((seq, SB_HD), lambda b, h, i: (b, 2 * nh + h)),
                  pl.BlockSpec((tq, SB_HD), lambda b, h, i: (b * nq + i, 3 * nh + h)),
                  vec, vec],
        out_specs=pl.BlockSpec((tq, SB_HD), lambda b, h, i: (b * nq + i, h)),
        out_shape=jax.ShapeDtypeStruct((bsz * seq, SB_WIDTH), BF16),
        scratch_shapes=[pltpu.VMEM((seq, SB_HD), BF16), pltpu.VMEM((seq, SB_HD), BF16),
                        pltpu.VMEM((tq, SB_HD), F32), pltpu.VMEM((tq, LANES), F32)],
        compiler_params=_params("parallel", "parallel", "arbitrary"),
        name="stickbreak",
    )(p, p, p, p, q_norm_g.reshape(1, SB_HD), k_norm_g.reshape(1, SB_HD))


def _even_layer(h, mkv, k_norm_g, norm_g, w_in, conv_qkv, a_log, dt_bias, dn_norm_g, dw_w, dw_b,
                ln_g, ln_b, q_norm_m, w_out, bsz, seq):
    c_ba = 4 * DN_WIDTH
    w_main = jnp.concatenate([w_in[:, :c_ba], w_in[:, c_ba + 2 * DN_HEADS:]], axis=1).astype(BF16)
    w_ba = jnp.pad(w_in[:, c_ba:c_ba + 2 * DN_HEADS], ((0, 0), (0, LANES - 2 * DN_HEADS))).astype(BF16)
    xn = _rmsnorm_bf16(h, norm_g)
    p_main = _proj([xn], [w_main])
    ba = _proj([xn], [w_ba])
    gates, gates_t = _gates(ba, a_log, dt_bias)
    o_a = _deltanet(p_main, gates, gates_t, conv_qkv, dn_norm_g, bsz, seq)
    o_b = _conformer(p_main, 4, 5, 6, dw_w, dw_b, ln_g, ln_b, bsz, seq)
    o_m = _mem_attend(p_main, 7 * MEM_HEADS, 8 * MEM_HEADS, mkv, q_norm_m, k_norm_g, bsz, seq)
    wo = w_out.astype(BF16)
    return _proj([o_a, o_b, o_m],
                 [wo[:DN_WIDTH], wo[DN_WIDTH:DN_WIDTH + CV_WIDTH], wo[DN_WIDTH + CV_WIDTH:]], res=h)


def _odd_layer(h, mkv, k_norm_g, norm_g, w_in, q_norm_c, k_norm_c, q_norm_m, w_out, bsz, seq):
    xn = _rmsnorm_bf16(h, norm_g)
    p = _proj([xn], [w_in.astype(BF16)])
    o_c = _stickbreak(p, q_norm_c, k_norm_c, bsz, seq)
    o_m = _mem_attend(p, 8 * MEM_HEADS, 9 * MEM_HEADS, mkv, q_norm_m, k_norm_g, bsz, seq)
    wo = w_out.astype(BF16)
    return _proj([o_c, o_m], [wo[:SB_WIDTH], wo[SB_WIDTH:]], res=h)


def kernel(x, mem, mem_norm_g, w_mem_kv, mem_k_norm_g, ev_norm_g, ev_w_in, ev_conv_qkv, ev_a_log,
           ev_dt_bias, ev_dn_norm_g, ev_dw_w, ev_dw_b, ev_ln_g, ev_ln_b, ev_q_norm_m, ev_w_out,
           od_norm_g, od_w_in, od_q_norm_c, od_k_norm_c, od_q_norm_m, od_w_out):
    bsz, seq, d = x.shape
    mlen = mem.shape[1]
    depth = ev_norm_g.shape[0] + od_norm_g.shape[0]
    mkv = _proj([_rmsnorm_bf16(mem.reshape(bsz * mlen, d), mem_norm_g, tm=256)],
                [w_mem_kv.astype(BF16)], tm=bsz * mlen)
    h = x.reshape(bsz * seq, d)
    for layer in range(depth):
        j = layer // 2
        if layer % 2 == 0:
            h = _even_layer(h, mkv, mem_k_norm_g, ev_norm_g[j], ev_w_in[j], ev_conv_qkv[j],
                            ev_a_log[j], ev_dt_bias[j], ev_dn_norm_g[j], ev_dw_w[j], ev_dw_b[j],
                            ev_ln_g[j], ev_ln_b[j], ev_q_norm_m[j], ev_w_out[j], bsz, seq)
        else:
            h = _odd_layer(h, mkv, mem_k_norm_g, od_norm_g[j], od_w_in[j], od_q_norm_c[j],
                           od_k_norm_c[j], od_q_norm_m[j], od_w_out[j], bsz, seq)
    return h.reshape(bsz, seq, d)
```

```python
import functools

import jax
import jax.numpy as jnp
from jax import lax
from jax.experimental import pallas as pl
from jax.experimental.pallas import tpu as pltpu

F32 = jnp.float32
BF16 = jnp.bfloat16
EPS = 1e-6
LOG2E = 1.4426950408889634

LANES = 128
SUBLANES = 8
VMEM_LIMIT = 48 * 1024 * 1024

DN_HEADS = 8
DN_HD = 128
DN_WIDTH = DN_HEADS * DN_HD
DN_CONV = 4
DN_CHUNK = 128
DN_SPLIT_MAX_POWER = 8
CV_WIDTH = 1024
CV_KERNEL = 31
CV_HALO = 32
SB_HEADS = 16
SB_HD = 128
SB_WIDTH = SB_HEADS * SB_HD
SB_TQ = 512
SB_TK = 128
MEM_HEADS = 4
MEM_HD = 256
MEM_WIDTH = MEM_HEADS * MEM_HD


def _params(*sem):
    return pltpu.CompilerParams(dimension_semantics=sem, vmem_limit_bytes=VMEM_LIMIT)


def _sigmoid(x):
    return 1.0 / (1.0 + jnp.exp(-x))


def _silu(x):
    return x * _sigmoid(x)


def _softplus(x):
    return jnp.maximum(x, 0.0) + jnp.log(1.0 + jnp.exp(-jnp.abs(x)))


def _mm(a, b):
    return jnp.dot(a.astype(BF16), b.astype(BF16), preferred_element_type=F32)


def _mm_nt(a, b):
    return lax.dot_general(a.astype(BF16), b.astype(BF16), (((1,), (1,)), ((), ())),
                           preferred_element_type=F32)


def _bmm(a, b):
    return lax.dot_general(a.astype(BF16), b.astype(BF16), (((2,), (1,)), ((0,), (0,))),
                           preferred_element_type=F32)


def _bmm3(a, b):
    a_hi = a.astype(BF16)
    b_hi = b.astype(BF16)
    a_lo = (a - a_hi.astype(F32)).astype(BF16)
    b_lo = (b - b_hi.astype(F32)).astype(BF16)
    lhs = jnp.concatenate([a_hi, a_hi, a_lo], axis=2)
    rhs = jnp.concatenate([b_hi, b_lo, b_hi], axis=1)
    return lax.dot_general(lhs, rhs, (((2,), (1,)), ((0,), (0,))), preferred_element_type=F32)


def _bmm_nt(a, b):
    return lax.dot_general(a.astype(BF16), b.astype(BF16), (((2,), (2,)), ((0,), (0,))),
                           preferred_element_type=F32)


def _rmsnorm_kernel(x_ref, g_ref, o_ref):
    x = x_ref[...]
    ms = jnp.mean(x * x, axis=-1, keepdims=True)
    o_ref[...] = (x * lax.rsqrt(ms + EPS) * g_ref[...]).astype(o_ref.dtype)


def _rmsnorm_bf16(x2d, g, tm=512):
    m, d = x2d.shape
    return pl.pallas_call(
        _rmsnorm_kernel,
        grid=(m // tm,),
        in_specs=[pl.BlockSpec((tm, d), lambda i: (i, 0)),
                  pl.BlockSpec((1, d), lambda i: (0, 0))],
        out_specs=pl.BlockSpec((tm, d), lambda i: (i, 0)),
        out_shape=jax.ShapeDtypeStruct((m, d), BF16),
        compiler_params=_params("parallel"),
        name="rmsnorm",
    )(x2d, g.reshape(1, d))


PROJ_CAST_ROWS = 256


def _proj_kernel(*refs, n_pairs, has_res, skip_from, skip_cols):
    j = pl.program_id(0)
    i = pl.program_id(1)
    xs = refs[:n_pairs]
    ws = refs[n_pairs:2 * n_pairs]
    pos = 2 * n_pairs
    nxt_ref = None
    if skip_from is not None:
        nxt_ref = refs[pos]
        pos += 1
    res_ref = refs[pos] if has_res else None
    pos += int(has_res)
    o_ref = refs[pos]
    wb = refs[pos + 1:]

    def cast(shifted):
        for p in range(n_pairs):
            src, dst = ws[p], wb[p]
            rows = min(PROJ_CAST_ROWS, src.shape[0])

            def body(r, carry, src=src, dst=dst, rows=rows):
                sl = pl.ds(pl.multiple_of(r * rows, rows), rows)
                w = src[sl, :]
                if shifted:
                    ext = jnp.concatenate([w, nxt_ref[sl, :]], axis=1)
                    w = pltpu.roll(ext, ext.shape[1] - skip_cols, axis=1)[:, :w.shape[1]]
                dst[sl, :] = w.astype(BF16)
                return carry

            lax.fori_loop(0, src.shape[0] // rows, body, 0)

    if skip_from is None:
        pl.when(i == 0)(lambda: cast(False))
    else:
        pl.when((i == 0) & (j < skip_from))(lambda: cast(False))
        pl.when((i == 0) & (j >= skip_from))(lambda: cast(True))

    acc = None
    for p in range(n_pairs):
        d = jnp.dot(xs[p][...], wb[p][...], preferred_element_type=F32)
        acc = d if acc is None else acc + d
    if has_res:
        acc = res_ref[...] + acc
    o_ref[...] = acc.astype(o_ref.dtype)


def _proj(xs, w, layer, row_blocks, n, col0=0, res=None, skip_from=None, skip_cols=0,
          tm=512, tn=1024):
    m = xs[0].shape[0]
    tm = min(tm, m)
    tn = min(tn, n)
    in_specs = [pl.BlockSpec((tm, x.shape[1]), lambda j, i: (i, 0)) for x in xs]
    in_specs += [pl.BlockSpec((None, x.shape[1], tn), lambda j, i, rb=rb: (layer, rb, col0 + j))
                 for x, rb in zip(xs, row_blocks)]
    args = list(xs) + [w] * len(xs)
    if skip_from is not None:
        k = xs[0].shape[1]
        in_specs.append(pl.BlockSpec((None, k, LANES),
                                     lambda j, i: (layer, 0, (col0 + j + 1) * (tn // LANES))))
        args.append(w)
    if res is not None:
        in_specs.append(pl.BlockSpec((tm, tn), lambda j, i: (i, j)))
        args.append(res)
    return pl.pallas_call(
        functools.partial(_proj_kernel, n_pairs=len(xs), has_res=res is not None,
                          skip_from=skip_from, skip_cols=skip_cols),
        grid=(n // tn, m // tm),
        in_specs=in_specs,
        out_specs=pl.BlockSpec((tm, tn), lambda j, i: (i, j)),
        out_shape=jax.ShapeDtypeStruct((m, n), F32),
        scratch_shapes=[pltpu.VMEM((x.shape[1], tn), BF16) for x in xs],
        compiler_params=_params("parallel", "arbitrary"),
        name="proj",
    )(*args)


def _gates_kernel(ba_ref, alog_ref, dt_ref, g_ref, gt_ref):
    x = ba_ref[...]
    col = lax.broadcasted_iota(jnp.int32, x.shape, 1)
    row = lax.broadcasted_iota(jnp.int32, x.shape, 0)
    beta = _sigmoid(x)
    g = -jnp.exp(alog_ref[...]) * _softplus(x + dt_ref[...])
    rin = row & (DN_CHUNK - 1)
    gc = g
    s = 1
    while s < DN_CHUNK:
        gc = gc + jnp.where(rin >= s, pltpu.roll(gc, s, axis=0), 0.0)
        s *= 2
    out = jnp.where(col < DN_HEADS, beta,
                    jnp.where(col < 2 * DN_HEADS, g, pltpu.roll(gc, DN_HEADS, axis=1)))
    g_ref[...] = out
    gt_ref[...] = out.T


def _gates(ba, a_log, dt_bias, ts=512):
    m = ba.shape[0]
    ts = min(ts, m)
    pad = lambda v: jnp.zeros((1, LANES), F32).at[0, DN_HEADS:2 * DN_HEADS].set(v.astype(F32))
    return pl.pallas_call(
        _gates_kernel,
        grid=(m // ts,),
        in_specs=[pl.BlockSpec((ts, LANES), lambda i: (i, 0)),
                  pl.BlockSpec((1, LANES), lambda i: (0, 0)),
                  pl.BlockSpec((1, LANES), lambda i: (0, 0))],
        out_specs=[pl.BlockSpec((ts, LANES), lambda i: (i, 0)),
                   pl.BlockSpec((LANES, ts), lambda i: (0, i))],
        out_shape=[jax.ShapeDtypeStruct((m, LANES), F32),
                   jax.ShapeDtypeStruct((LANES, m), F32)],
        compiler_params=_params("parallel"),
        name="dn_gates",
    )(ba, pad(a_log), pad(dt_bias))


def _dn_kernel(q_ref, k_ref, v_ref, hq_ref, hk_ref, hv_ref, z_ref, g_ref, gt_ref,
               wq_ref, wk_ref, wv_ref, ng_ref, o_ref, state_ref):
    i = pl.program_id(1)
    c = DN_CHUNK

    @pl.when(i == 0)
    def _():
        state_ref[...] = jnp.zeros_like(state_ref)

    def conv_act(cur_ref, halo_ref, w_ref):
        halo = jnp.where(i == 0, 0.0, halo_ref[...])
        x = jnp.concatenate([halo, cur_ref[...]], axis=0)
        w = w_ref[...]
        y = x * w[DN_CONV - 1:DN_CONV, :]
        for tap in range(DN_CONV - 1):
            y = y + pltpu.roll(x, DN_CONV - 1 - tap, axis=0) * w[tap:tap + 1, :]
        return _silu(y[SUBLANES:, :])

    heads = lambda t: jnp.stack([t[:, h * DN_HD:(h + 1) * DN_HD] for h in range(DN_HEADS)])
    q = heads(conv_act(q_ref, hq_ref, wq_ref))
    k = heads(conv_act(k_ref, hk_ref, wk_ref))
    v = heads(conv_act(v_ref, hv_ref, wv_ref))
    q = q * lax.rsqrt(jnp.sum(q * q, axis=-1, keepdims=True) + EPS) * (DN_HD ** -0.5)
    k = k * lax.rsqrt(jnp.sum(k * k, axis=-1, keepdims=True) + EPS)
    gates = g_ref[...]
    gates_t = gt_ref[...]
    beta = jnp.stack([gates[:, h:h + 1] for h in range(DN_HEADS)])
    gcol = jnp.stack([gates[:, 2 * DN_HEADS + h:2 * DN_HEADS + h + 1] for h in range(DN_HEADS)])
    grow = jnp.stack([gates_t[2 * DN_HEADS + h:2 * DN_HEADS + h + 1, :] for h in range(DN_HEADS)])
    glast = grow[:, :, c - 1:c]

    ri = lax.broadcasted_iota(jnp.int32, (DN_HEADS, c, c), 1)
    ci = lax.broadcasted_iota(jnp.int32, (DN_HEADS, c, c), 2)
    decay = jnp.exp(jnp.minimum(gcol - grow, 0.0))
    kb = k * beta
    s2 = _bmm_nt(jnp.concatenate([kb, q], axis=1), k)
    a_mat = jnp.where(ri > ci, s2[:, :c] * decay, 0.0)
    qk = jnp.where(ri >= ci, s2[:, c:] * decay, 0.0)
    x = _bmm3(a_mat, a_mat)
    n = -a_mat
    p = 2
    while 2 * p < c:
        mm = _bmm3 if p <= DN_SPLIT_MAX_POWER else _bmm
        both = mm(x, jnp.concatenate([x, n], axis=-1))
        n = n + x + both[:, :, c:]
        x = both[:, :, :c]
        p *= 2
    n = n + x + _bmm(x, n)
    egc = jnp.exp(gcol)
    rhs = jnp.concatenate([v * beta, kb * egc], axis=-1)
    sol = rhs + _bmm(n, rhs)
    u_val, w_key = sol[:, :, :DN_HD], sol[:, :, DN_HD:]
    state = state_ref[...]
    both = _bmm(jnp.concatenate([w_key, q * egc], axis=1), state)
    v_new = u_val - both[:, :c]
    kd_t = jnp.swapaxes(k * jnp.exp(glast - gcol), 1, 2)
    tail = _bmm(jnp.concatenate([qk, kd_t], axis=1), v_new)
    o = both[:, c:] + tail[:, :c]
    state_ref[...] = state * jnp.exp(glast) + tail[:, c:]
    o = o * lax.rsqrt(jnp.mean(o * o, axis=-1, keepdims=True) + EPS) * ng_ref[...]
    z_all = z_ref[...]
    for h in range(DN_HEADS):
        sl = slice(h * DN_HD, (h + 1) * DN_HD)
        o_ref[:, sl] = (o[h] * _silu(z_all[:, sl])).astype(o_ref.dtype)


def _deltanet(p_main, gates, gates_t, conv_w, norm_g, bsz, seq):
    c = DN_CHUNK
    ns = seq // c
    row = lambda b, i: b * ns + i
    halo_row = lambda b, i: jnp.maximum((b * ns + i) * (c // SUBLANES) - 1, 0)
    cur = lambda cb: pl.BlockSpec((c, DN_WIDTH), lambda b, i: (row(b, i), cb))
    halo = lambda cb: pl.BlockSpec((SUBLANES, DN_WIDTH), lambda b, i: (halo_row(b, i), cb))
    wspec = lambda cb: pl.BlockSpec((DN_CONV, DN_WIDTH), lambda b, i: (0, cb))
    return pl.pallas_call(
        _dn_kernel,
        grid=(bsz, ns),
        in_specs=[cur(0), cur(1), cur(2), halo(0), halo(1), halo(2), cur(3),
                  pl.BlockSpec((c, LANES), lambda b, i: (row(b, i), 0)),
                  pl.BlockSpec((LANES, c), lambda b, i: (0, row(b, i))),
                  wspec(0), wspec(1), wspec(2),
                  pl.BlockSpec((1, DN_HD), lambda b, i: (0, 0))],
        out_specs=pl.BlockSpec((c, DN_WIDTH), lambda b, i: (row(b, i), 0)),
        out_shape=jax.ShapeDtypeStruct((bsz * seq, DN_WIDTH), BF16),
        scratch_shapes=[pltpu.VMEM((DN_HEADS, DN_HD, DN_HD), F32)],
        compiler_params=_params("parallel", "arbitrary"),
        name="deltanet",
    )(p_main, p_main, p_main, p_main, p_main, p_main, p_main, gates, gates_t,
      conv_w, conv_w, conv_w, norm_g.reshape(1, DN_HD))


def _cv_kernel(a_ref, b_ref, ha_ref, hb_ref, z_ref, w_ref, bias_ref, lg_ref, lb_ref, o_ref):
    i = pl.program_id(1)
    ts = a_ref.shape[0]
    glu_cur = a_ref[...] * _sigmoid(b_ref[...])
    glu_halo = jnp.where(i == 0, 0.0, ha_ref[...] * _sigmoid(hb_ref[...]))
    x = jnp.concatenate([glu_halo, glu_cur], axis=0)
    w = w_ref[...]
    shifted = [x] + [pltpu.roll(x, r, axis=0) for r in range(1, SUBLANES)]
    acc = None
    for tap in range(CV_KERNEL):
        back = CV_KERNEL - 1 - tap
        start = CV_HALO - (back // SUBLANES) * SUBLANES
        term = shifted[back % SUBLANES][start:start + ts, :] * w[tap:tap + 1, :]
        acc = term if acc is None else acc + term
    y = acc + bias_ref[...]
    mu = jnp.mean(y, axis=-1, keepdims=True)
    yc = y - mu
    y = yc * lax.rsqrt(jnp.mean(yc * yc, axis=-1, keepdims=True) + EPS) * lg_ref[...] + lb_ref[...]
    o_ref[...] = (_silu(y) * _silu(z_ref[...])).astype(o_ref.dtype)


def _conformer(p_main, cb_a, cb_b, cb_z, dw_w, dw_b, ln_g, ln_b, bsz, seq, ts=256):
    ts = min(ts, seq)
    ns = seq // ts
    row = lambda b, i: b * ns + i
    halo_row = lambda b, i: jnp.maximum((b * ns + i) * (ts // CV_HALO) - 1, 0)
    cur = lambda cb: pl.BlockSpec((ts, CV_WIDTH), lambda b, i: (row(b, i), cb))
    halo = lambda cb: pl.BlockSpec((CV_HALO, CV_WIDTH), lambda b, i: (halo_row(b, i), cb))
    vec = pl.BlockSpec((1, CV_WIDTH), lambda b, i: (0, 0))
    return pl.pallas_call(
        _cv_kernel,
        grid=(bsz, ns),
        in_specs=[cur(cb_a), cur(cb_b), halo(cb_a), halo(cb_b), cur(cb_z),
                  pl.BlockSpec((CV_KERNEL, CV_WIDTH), lambda b, i: (0, 0)), vec, vec, vec],
        out_specs=pl.BlockSpec((ts, CV_WIDTH), lambda b, i: (row(b, i), 0)),
        out_shape=jax.ShapeDtypeStruct((bsz * seq, CV_WIDTH), BF16),
        compiler_params=_params("parallel", "parallel"),
        name="conformer_conv",
    )(p_main, p_main, p_main, p_main, p_main, dw_w,
      dw_b.reshape(1, CV_WIDTH), ln_g.reshape(1, CV_WIDTH), ln_b.reshape(1, CV_WIDTH))


def _mem_kernel(q_ref, z_ref, k_ref, v_ref, qg_ref, kg_ref, o_ref):
    q = q_ref[...]
    qn = q * lax.rsqrt(jnp.mean(q * q, axis=-1, keepdims=True) + EPS) * qg_ref[...]
    k = k_ref[...]
    kn = k * lax.rsqrt(jnp.mean(k * k, axis=-1, keepdims=True) + EPS) * kg_ref[...]
    s = _mm_nt(qn * (MEM_HD ** -0.5), kn)
    p = jnp.exp(s - jnp.max(s, axis=-1, keepdims=True))
    o = _mm(p, v_ref[...]) / jnp.sum(p, axis=-1, keepdims=True)
    o_ref[...] = (o * _silu(z_ref[...])).astype(o_ref.dtype)


def _mem_attend(p, cb_q, cb_z, mkv, q_norm_g, k_norm_g, bsz, seq, ts=512):
    ts = min(ts, seq)
    ns = seq // ts
    mlen = mkv.shape[0] // bsz
    row = lambda b, h, i: b * ns + i
    vec = pl.BlockSpec((1, MEM_HD), lambda b, h, i: (0, 0))
    return pl.pallas_call(
        _mem_kernel,
        grid=(bsz, MEM_HEADS, ns),
        in_specs=[pl.BlockSpec((ts, MEM_HD), lambda b, h, i: (row(b, h, i), cb_q + h)),
                  pl.BlockSpec((ts, MEM_HD), lambda b, h, i: (row(b, h, i), cb_z + h)),
                  pl.BlockSpec((mlen, MEM_HD), lambda b, h, i: (b, h)),
                  pl.BlockSpec((mlen, MEM_HD), lambda b, h, i: (b, MEM_HEADS + h)),
                  vec, vec],
        out_specs=pl.BlockSpec((ts, MEM_HD), lambda b, h, i: (row(b, h, i), h)),
        out_shape=jax.ShapeDtypeStruct((bsz * seq, MEM_WIDTH), BF16),
        compiler_params=_params("parallel", "parallel", "parallel"),
        name="mem_attend",
    )(p, p, mkv, mkv, q_norm_g.reshape(1, MEM_HD), k_norm_g.reshape(1, MEM_HD))


def _sb_kernel(q_ref, k_ref, v_ref, z_ref, qg_ref, kg_ref, o_ref, kn_ref, vb_ref, *, nq):
    i = pl.program_id(2)
    tk = SB_TK
    tq = q_ref.shape[0]

    @pl.when(i == 0)
    def _():
        k = k_ref[...]
        kn = k * lax.rsqrt(jnp.mean(k * k, axis=-1, keepdims=True) + EPS) * kg_ref[...]
        kn_ref[...] = kn.astype(BF16)
        vb_ref[...] = v_ref[...].astype(BF16)

    q = q_ref[...]
    qn = q * lax.rsqrt(jnp.mean(q * q, axis=-1, keepdims=True) + EPS) * qg_ref[...]
    qn = (qn * (SB_HD ** -0.5 * LOG2E)).astype(BF16)
    nsub = tq // tk

    uj = lax.broadcasted_iota(jnp.int32, (2 * tk, tk + LANES), 0) & (tk - 1)
    us = lax.broadcasted_iota(jnp.int32, (2 * tk, tk + LANES), 1)
    u_aug = jnp.where((uj >= us) | (us >= tk), 1.0, 0.0).astype(BF16)

    sign_bit = jnp.uint32(0x80000000)

    def sub_block(z, vis):
        neg_abs = lax.bitcast_convert_type(lax.bitcast_convert_type(z, jnp.uint32) | sign_bit, F32)
        cost = jnp.maximum(z, 0.0) + jnp.log2(1.0 + jnp.exp2(neg_abs))
        if vis is not None:
            cost = jnp.where(vis, cost, 0.0)
        hi = cost.astype(BF16)
        lo = (cost - hi.astype(F32)).astype(BF16)
        inc = jnp.dot(jnp.concatenate([hi, lo], axis=1), u_aug, preferred_element_type=F32)
        return z - inc[:, :tk], inc[:, tk:]

    def key_block(key_start, run, diagonal):
        kb = kn_ref[pl.ds(key_start, tq), :]
        vb = vb_ref[pl.ds(key_start, tq), :]
        z_all = lax.dot_general(qn, kb, (((1,), (1,)), ((), ())), preferred_element_type=F32)
        ws = [None] * nsub
        for s in range(nsub - 1, -1, -1):
            vis = None
            if diagonal:
                vis = (lax.broadcasted_iota(jnp.int32, (tq, tk), 1) + s * tk
                       < lax.broadcasted_iota(jnp.int32, (tq, tk), 0))
            pre, tot = sub_block(z_all[:, s * tk:(s + 1) * tk], vis)
            w = jnp.exp2(pre + run)
            if diagonal:
                w = jnp.where(vis, w, 0.0)
            ws[s] = w.astype(BF16)
            run = run - tot
        return jnp.dot(jnp.concatenate(ws, axis=1), vb, preferred_element_type=F32), run

    def query_block(qi):
        run = jnp.zeros((tq, LANES), F32)
        acc = None
        for kj in range(qi, -1, -1):
            pv, run = key_block(kj * tq, run, kj == qi)
            acc = pv if acc is None else acc + pv
        o_ref[...] = (acc * _silu(z_ref[...])).astype(o_ref.dtype)

    for qi in range(nq):
        pl.when(i == qi)(functools.partial(query_block, qi))


def _stickbreak(p, q_norm_g, k_norm_g, bsz, seq):
    tq = min(SB_TQ, seq)
    nq = seq // tq
    nh = SB_HEADS
    vec = pl.BlockSpec((1, SB_HD), lambda b, h, i: (0, 0))
    return pl.pallas_call(
        functools.partial(_sb_kernel, nq=nq),
        grid=(bsz, nh, nq),
        in_specs=[pl.BlockSpec((tq, SB_HD), lambda b, h, i: (b * nq + i, h)),
                  pl.BlockSpec((seq, SB_HD), lambda b, h, i: (b, nh + h)),
                  pl.BlockSpec((seq, SB_HD), lambda b, h, i: (b, 2 * nh + h)),
                  pl.BlockSpec((tq, SB_HD), lambda b, h, i: (b * nq + i, 3 * nh + h)),
                  vec, vec],
        out_specs=pl.BlockSpec((tq, SB_HD), lambda b, h, i: (b * nq + i, h)),
        out_shape=jax.ShapeDtypeStruct((bsz * seq, SB_WIDTH), BF16),
        scratch_shapes=[pltpu.VMEM((seq, SB_HD), BF16), pltpu.VMEM((seq, SB_HD), BF16)],
        compiler_params=_params("parallel", "parallel", "arbitrary"),
        name="stickbreak",
    )(p, p, p, p, q_norm_g.reshape(1, SB_HD), k_norm_g.reshape(1, SB_HD))


def _even_layer(h, mkv, k_norm_g, j, norm_g, w_in, conv_qkv, a_log, dt_bias, dn_norm_g, dw_w, dw_b,
                ln_g, ln_b, q_norm_m, w_out, bsz, seq):
    d_model = h.shape[1]
    xn = _rmsnorm_bf16(h, norm_g)
    n_main = w_in.shape[2] - 2 * DN_HEADS
    p_main = _proj([xn], w_in, j, [0], n_main, skip_from=4 * DN_WIDTH // 1024,
                   skip_cols=2 * DN_HEADS, tn=1024)
    ba = _proj([xn], w_in, j, [0], LANES, col0=4 * DN_WIDTH // LANES, tn=LANES)
    gates, gates_t = _gates(ba, a_log, dt_bias)
    o_a = _deltanet(p_main, gates, gates_t, conv_qkv, dn_norm_g, bsz, seq)
    o_b = _conformer(p_main, 4, 5, 6, dw_w, dw_b, ln_g, ln_b, bsz, seq)
    o_m = _mem_attend(p_main, 7 * MEM_HEADS, 8 * MEM_HEADS, mkv, q_norm_m, k_norm_g, bsz, seq)
    return _proj([o_a, o_b, o_m], w_out, j, [0, 1, 2], d_model, res=h, tn=512)


def _odd_layer(h, mkv, k_norm_g, j, norm_g, w_in, q_norm_c, k_norm_c, q_norm_m, w_out, bsz, seq):
    d_model = h.shape[1]
    xn = _rmsnorm_bf16(h, norm_g)
    p = _proj([xn], w_in, j, [0], w_in.shape[2], tn=1024)
    o_c = _stickbreak(p, q_norm_c, k_norm_c, bsz, seq)
    o_m = _mem_attend(p, 8 * MEM_HEADS, 9 * MEM_HEADS, mkv, q_norm_m, k_norm_g, bsz, seq)
    return _proj([o_c, o_m], w_out, j, [0, SB_WIDTH // MEM_WIDTH], d_model, res=h, tn=512)


def kernel(x, mem, mem_norm_g, w_mem_kv, mem_k_norm_g, ev_norm_g, ev_w_in, ev_conv_qkv, ev_a_log,
           ev_dt_bias, ev_dn_norm_g, ev_dw_w, ev_dw_b, ev_ln_g, ev_ln_b, ev_q_norm_m, ev_w_out,
           od_norm_g, od_w_in, od_q_norm_c, od_k_norm_c, od_q_norm_m, od_w_out):
    bsz, seq, d = x.shape
    mlen = mem.shape[1]
    depth = ev_norm_g.shape[0] + od_norm_g.shape[0]
    mkv = _proj([_rmsnorm_bf16(mem.reshape(bsz * mlen, d), mem_norm_g, tm=256)],
                w_mem_kv[None], 0, [0], w_mem_kv.shape[1], tm=bsz * mlen, tn=1024)
    h = x.reshape(bsz * seq, d)
    for layer in range(depth):
        j = layer // 2
        if layer % 2 == 0:
            h = _even_layer(h, mkv, mem_k_norm_g, j, ev_norm_g[j], ev_w_in, ev_conv_qkv[j],
                            ev_a_log[j], ev_dt_bias[j], ev_dn_norm_g[j], ev_dw_w[j], ev_dw_b[j],
                            ev_ln_g[j], ev_ln_b[j], ev_q_norm_m[j], ev_w_out, bsz, seq)
        else:
            h = _odd_layer(h, mkv, mem_k_norm_g, j, od_norm_g[j], od_w_in, od_q_norm_c[j],
                           od_k_norm_c[j], od_q_norm_m[j], od_w_out, bsz, seq)
    return h.reshape(bsz, seq, d)
```

```python
import functools

import jax
import jax.numpy as jnp
from jax import lax
from jax.experimental import pallas as pl
from jax.experimental.pallas import tpu as pltpu

F32 = jnp.float32
BF16 = jnp.bfloat16
EPS = 1e-6
LOG2E = 1.4426950408889634

LANES = 128
SUBLANES = 8
VMEM_LIMIT = 48 * 1024 * 1024

DN_HEADS = 8
DN_HD = 128
DN_WIDTH = DN_HEADS * DN_HD
DN_CONV = 4
DN_CHUNK = 128
DN_SPLIT_MAX_POWER = 8
CV_WIDTH = 1024
CV_KERNEL = 31
CV_HALO = 32
SB_HEADS = 16
SB_HD = 128
SB_WIDTH = SB_HEADS * SB_HD
SB_TQ = 512
SB_TK = 128
MEM_HEADS = 4
MEM_HD = 256
MEM_WIDTH = MEM_HEADS * MEM_HD


def _params(*sem):
    return pltpu.CompilerParams(dimension_semantics=sem, vmem_limit_bytes=VMEM_LIMIT)


def _sigmoid(x):
    return 1.0 / (1.0 + jnp.exp(-x))


def _silu(x):
    return x * _sigmoid(x)


def _softplus(x):
    return jnp.maximum(x, 0.0) + jnp.log(1.0 + jnp.exp(-jnp.abs(x)))


def _mm(a, b):
    return jnp.dot(a.astype(BF16), b.astype(BF16), preferred_element_type=F32)


def _mm_nt(a, b):
    return lax.dot_general(a.astype(BF16), b.astype(BF16), (((1,), (1,)), ((), ())),
                           preferred_element_type=F32)


def _bmm(a, b):
    return lax.dot_general(a.astype(BF16), b.astype(BF16), (((2,), (1,)), ((0,), (0,))),
                           preferred_element_type=F32)


def _bmm3(a, b):
    a_hi = a.astype(BF16)
    b_hi = b.astype(BF16)
    a_lo = (a - a_hi.astype(F32)).astype(BF16)
    b_lo = (b - b_hi.astype(F32)).astype(BF16)
    lhs = jnp.concatenate([a_hi, a_hi, a_lo], axis=2)
    rhs = jnp.concatenate([b_hi, b_lo, b_hi], axis=1)
    return lax.dot_general(lhs, rhs, (((2,), (1,)), ((0,), (0,))), preferred_element_type=F32)


def _bmm_nt(a, b):
    return lax.dot_general(a.astype(BF16), b.astype(BF16), (((2,), (2,)), ((0,), (0,))),
                           preferred_element_type=F32)


def _rmsnorm_kernel(x_ref, g_ref, o_ref):
    x = x_ref[...]
    ms = jnp.mean(x * x, axis=-1, keepdims=True)
    o_ref[...] = (x * lax.rsqrt(ms + EPS) * g_ref[...]).astype(o_ref.dtype)


def _rmsnorm_bf16(x2d, g, tm=512):
    m, d = x2d.shape
    return pl.pallas_call(
        _rmsnorm_kernel,
        grid=(m // tm,),
        in_specs=[pl.BlockSpec((tm, d), lambda i: (i, 0)),
                  pl.BlockSpec((1, d), lambda i: (0, 0))],
        out_specs=pl.BlockSpec((tm, d), lambda i: (i, 0)),
        out_shape=jax.ShapeDtypeStruct((m, d), BF16),
        compiler_params=_params("parallel"),
        name="rmsnorm",
    )(x2d, g.reshape(1, d))


PROJ_CAST_ROWS = 256


def _cast_rows(src, dst, offset, tail):
    nrows = src.shape[0]
    rows = min(PROJ_CAST_ROWS, nrows)
    nchunks = nrows // rows

    def body(r, carry):
        start = r * rows
        dst[pl.ds(pl.multiple_of(start, rows), rows), :] = (
            src[pl.ds(pl.multiple_of(start + offset, SUBLANES), rows), :].astype(BF16))
        return carry

    lax.fori_loop(0, nchunks - 1 if offset else nchunks, body, 0)
    if offset:
        last = (nchunks - 1) * rows
        dst[last:nrows - offset, :] = src[last + offset:, :].astype(BF16)
        dst[nrows - offset:, :] = tail[...].astype(BF16)


def _proj_kernel(*refs, n_pairs, has_res, transposed, skip_from, skip):
    j = pl.program_id(0)
    i = pl.program_id(1)
    xs = refs[:n_pairs]
    ws = refs[n_pairs:2 * n_pairs]
    pos = 2 * n_pairs
    tail_ref = None
    if skip_from is not None:
        tail_ref = refs[pos]
        pos += 1
    res_ref = refs[pos] if has_res else None
    pos += int(has_res)
    o_ref = refs[pos]
    wb = refs[pos + 1:]

    def cast(offset):
        for p in range(n_pairs):
            _cast_rows(ws[p], wb[p], offset, tail_ref)

    if skip_from is None:
        pl.when(i == 0)(lambda: cast(0))
    else:
        pl.when((i == 0) & (j < skip_from))(lambda: cast(0))
        pl.when((i == 0) & (j >= skip_from))(lambda: cast(skip))

    contract = (((1,), (1,)), ((), ())) if transposed else (((1,), (0,)), ((), ()))
    acc = None
    for p in range(n_pairs):
        d = lax.dot_general(xs[p][...], wb[p][...], contract, preferred_element_type=F32)
        acc = d if acc is None else acc + d
    if has_res:
        acc = res_ref[...] + acc
    o_ref[...] = acc.astype(o_ref.dtype)


def _proj(xs, w, layer, row_blocks, n, col0=0, res=None, transposed=False, skip_from=None, skip=0,
          tm=1024, tn=1024):
    m = xs[0].shape[0]
    tm = min(tm, m)
    tn = min(tn, n)
    in_specs = [pl.BlockSpec((tm, x.shape[1]), lambda j, i: (i, 0)) for x in xs]
    if transposed:
        k = xs[0].shape[1]
        in_specs.append(pl.BlockSpec((None, tn, k), lambda j, i: (layer, col0 + j, 0)))
        scratch = [pltpu.VMEM((tn, k), BF16)]
    else:
        in_specs += [pl.BlockSpec((None, x.shape[1], tn), lambda j, i, rb=rb: (layer, rb, col0 + j))
                     for x, rb in zip(xs, row_blocks)]
        scratch = [pltpu.VMEM((x.shape[1], tn), BF16) for x in xs]
    args = list(xs) + [w] * len(xs)
    if skip_from is not None:
        in_specs.append(pl.BlockSpec((None, skip, xs[0].shape[1]),
                                     lambda j, i: (layer, (col0 + j + 1) * (tn // skip), 0)))
        args.append(w)
    if res is not None:
        in_specs.append(pl.BlockSpec((tm, tn), lambda j, i: (i, j)))
        args.append(res)
    return pl.pallas_call(
        functools.partial(_proj_kernel, n_pairs=len(xs), has_res=res is not None,
                          transposed=transposed, skip_from=skip_from, skip=skip),
        grid=(n // tn, m // tm),
        in_specs=in_specs,
        out_specs=pl.BlockSpec((tm, tn), lambda j, i: (i, j)),
        out_shape=jax.ShapeDtypeStruct((m, n), F32),
        scratch_shapes=scratch,
        compiler_params=_params("parallel", "arbitrary"),
        name="proj",
    )(*args)


def _gates_kernel(ba_ref, alog_ref, dt_ref, g_ref, gt_ref):
    x = ba_ref[...]
    col = lax.broadcasted_iota(jnp.int32, x.shape, 1)
    row = lax.broadcasted_iota(jnp.int32, x.shape, 0)
    beta = _sigmoid(x)
    g = -jnp.exp(alog_ref[...]) * _softplus(x + dt_ref[...])
    rin = row & (DN_CHUNK - 1)
    gc = g
    s = 1
    while s < DN_CHUNK:
        gc = gc + jnp.where(rin >= s, pltpu.roll(gc, s, axis=0), 0.0)
        s *= 2
    out = jnp.where(col < DN_HEADS, beta,
                    jnp.where(col < 2 * DN_HEADS, g, pltpu.roll(gc, DN_HEADS, axis=1)))
    g_ref[...] = out
    gt_ref[...] = out.T


def _gates(ba, a_log, dt_bias, ts=512):
    m = ba.shape[0]
    ts = min(ts, m)
    pad = lambda v: jnp.zeros((1, LANES), F32).at[0, DN_HEADS:2 * DN_HEADS].set(v.astype(F32))
    return pl.pallas_call(
        _gates_kernel,
        grid=(m // ts,),
        in_specs=[pl.BlockSpec((ts, LANES), lambda i: (i, 0)),
                  pl.BlockSpec((1, LANES), lambda i: (0, 0)),
                  pl.BlockSpec((1, LANES), lambda i: (0, 0))],
        out_specs=[pl.BlockSpec((ts, LANES), lambda i: (i, 0)),
                   pl.BlockSpec((LANES, ts), lambda i: (0, i))],
        out_shape=[jax.ShapeDtypeStruct((m, LANES), F32),
                   jax.ShapeDtypeStruct((LANES, m), F32)],
        compiler_params=_params("parallel"),
        name="dn_gates",
    )(ba, pad(a_log), pad(dt_bias))


def _dn_kernel(q_ref, k_ref, v_ref, hq_ref, hk_ref, hv_ref, z_ref, g_ref, gt_ref,
               wq_ref, wk_ref, wv_ref, ng_ref, o_ref, state_ref):
    i = pl.program_id(1)
    c = DN_CHUNK

    @pl.when(i == 0)
    def _():
        state_ref[...] = jnp.zeros_like(state_ref)

    def conv_act(cur_ref, halo_ref, w_ref):
        halo = jnp.where(i == 0, 0.0, halo_ref[...])
        x = jnp.concatenate([halo, cur_ref[...]], axis=0)
        w = w_ref[...]
        y = x * w[DN_CONV - 1:DN_CONV, :]
        for tap in range(DN_CONV - 1):
            y = y + pltpu.roll(x, DN_CONV - 1 - tap, axis=0) * w[tap:tap + 1, :]
        return _silu(y[SUBLANES:, :])

    heads = lambda t: jnp.stack([t[:, h * DN_HD:(h + 1) * DN_HD] for h in range(DN_HEADS)])
    q = heads(conv_act(q_ref, hq_ref, wq_ref))
    k = heads(conv_act(k_ref, hk_ref, wk_ref))
    v = heads(conv_act(v_ref, hv_ref, wv_ref))
    q = q * lax.rsqrt(jnp.sum(q * q, axis=-1, keepdims=True) + EPS) * (DN_HD ** -0.5)
    k = k * lax.rsqrt(jnp.sum(k * k, axis=-1, keepdims=True) + EPS)
    gates = g_ref[...]
    gates_t = gt_ref[...]
    beta = jnp.stack([gates[:, h:h + 1] for h in range(DN_HEADS)])
    gcol = jnp.stack([gates[:, 2 * DN_HEADS + h:2 * DN_HEADS + h + 1] for h in range(DN_HEADS)])
    grow = jnp.stack([gates_t[2 * DN_HEADS + h:2 * DN_HEADS + h + 1, :] for h in range(DN_HEADS)])
    glast = grow[:, :, c - 1:c]

    ri = lax.broadcasted_iota(jnp.int32, (DN_HEADS, c, c), 1)
    ci = lax.broadcasted_iota(jnp.int32, (DN_HEADS, c, c), 2)
    decay = jnp.exp(jnp.minimum(gcol - grow, 0.0))
    kb = k * beta
    s2 = _bmm_nt(jnp.concatenate([kb, q], axis=1), k)
    a_mat = jnp.where(ri > ci, s2[:, :c] * decay, 0.0)
    qk = jnp.where(ri >= ci, s2[:, c:] * decay, 0.0)
    x = _bmm3(a_mat, a_mat)
    n = -a_mat
    p = 2
    while 2 * p < c:
        mm = _bmm3 if p <= DN_SPLIT_MAX_POWER else _bmm
        both = mm(x, jnp.concatenate([x, n], axis=-1))
        n = n + x + both[:, :, c:]
        x = both[:, :, :c]
        p *= 2
    n = n + x + _bmm(x, n)
    egc = jnp.exp(gcol)
    rhs = jnp.concatenate([v * beta, kb * egc], axis=-1)
    sol = rhs + _bmm(n, rhs)
    u_val, w_key = sol[:, :, :DN_HD], sol[:, :, DN_HD:]
    state = state_ref[...]
    both = _bmm(jnp.concatenate([w_key, q * egc], axis=1), state)
    v_new = u_val - both[:, :c]
    kd_t = jnp.swapaxes(k * jnp.exp(glast - gcol), 1, 2)
    tail = _bmm(jnp.concatenate([qk, kd_t], axis=1), v_new)
    o = both[:, c:] + tail[:, :c]
    state_ref[...] = state * jnp.exp(glast) + tail[:, c:]
    o = o * lax.rsqrt(jnp.mean(o * o, axis=-1, keepdims=True) + EPS) * ng_ref[...]
    z_all = z_ref[...]
    for h in range(DN_HEADS):
        sl = slice(h * DN_HD, (h + 1) * DN_HD)
        o_ref[:, sl] = (o[h] * _silu(z_all[:, sl])).astype(o_ref.dtype)


def _deltanet(p_main, gates, gates_t, conv_w, norm_g, bsz, seq):
    c = DN_CHUNK
    ns = seq // c
    row = lambda b, i: b * ns + i
    halo_row = lambda b, i: jnp.maximum((b * ns + i) * (c // SUBLANES) - 1, 0)
    cur = lambda cb: pl.BlockSpec((c, DN_WIDTH), lambda b, i: (row(b, i), cb))
    halo = lambda cb: pl.BlockSpec((SUBLANES, DN_WIDTH), lambda b, i: (halo_row(b, i), cb))
    wspec = lambda cb: pl.BlockSpec((DN_CONV, DN_WIDTH), lambda b, i: (0, cb))
    return pl.pallas_call(
        _dn_kernel,
        grid=(bsz, ns),
        in_specs=[cur(0), cur(1), cur(2), halo(0), halo(1), halo(2), cur(3),
                  pl.BlockSpec((c, LANES), lambda b, i: (row(b, i), 0)),
                  pl.BlockSpec((LANES, c), lambda b, i: (0, row(b, i))),
                  wspec(0), wspec(1), wspec(2),
                  pl.BlockSpec((1, DN_HD), lambda b, i: (0, 0))],
        out_specs=pl.BlockSpec((c, DN_WIDTH), lambda b, i: (row(b, i), 0)),
        out_shape=jax.ShapeDtypeStruct((bsz * seq, DN_WIDTH), BF16),
        scratch_shapes=[pltpu.VMEM((DN_HEADS, DN_HD, DN_HD), F32)],
        compiler_params=_params("parallel", "arbitrary"),
        name="deltanet",
    )(p_main, p_main, p_main, p_main, p_main, p_main, p_main, gates, gates_t,
      conv_w, conv_w, conv_w, norm_g.reshape(1, DN_HD))


def _cv_kernel(a_ref, b_ref, ha_ref, hb_ref, z_ref, w_ref, bias_ref, lg_ref, lb_ref, o_ref):
    i = pl.program_id(1)
    ts = a_ref.shape[0]
    glu_cur = a_ref[...] * _sigmoid(b_ref[...])
    glu_halo = jnp.where(i == 0, 0.0, ha_ref[...] * _sigmoid(hb_ref[...]))
    x = jnp.concatenate([glu_halo, glu_cur], axis=0)
    w = w_ref[...]
    shifted = [x] + [pltpu.roll(x, r, axis=0) for r in range(1, SUBLANES)]
    acc = None
    for tap in range(CV_KERNEL):
        back = CV_KERNEL - 1 - tap
        start = CV_HALO - (back // SUBLANES) * SUBLANES
        term = shifted[back % SUBLANES][start:start + ts, :] * w[tap:tap + 1, :]
        acc = term if acc is None else acc + term
    y = acc + bias_ref[...]
    mu = jnp.mean(y, axis=-1, keepdims=True)
    yc = y - mu
    y = yc * lax.rsqrt(jnp.mean(yc * yc, axis=-1, keepdims=True) + EPS) * lg_ref[...] + lb_ref[...]
    o_ref[...] = (_silu(y) * _silu(z_ref[...])).astype(o_ref.dtype)


def _conformer(p_main, cb_a, cb_b, cb_z, dw_w, dw_b, ln_g, ln_b, bsz, seq, ts=256):
    ts = min(ts, seq)
    ns = seq // ts
    row = lambda b, i: b * ns + i
    halo_row = lambda b, i: jnp.maximum((b * ns + i) * (ts // CV_HALO) - 1, 0)
    cur = lambda cb: pl.BlockSpec((ts, CV_WIDTH), lambda b, i: (row(b, i), cb))
    halo = lambda cb: pl.BlockSpec((CV_HALO, CV_WIDTH), lambda b, i: (halo_row(b, i), cb))
    vec = pl.BlockSpec((1, CV_WIDTH), lambda b, i: (0, 0))
    return pl.pallas_call(
        _cv_kernel,
        grid=(bsz, ns),
        in_specs=[cur(cb_a), cur(cb_b), halo(cb_a), halo(cb_b), cur(cb_z),
                  pl.BlockSpec((CV_KERNEL, CV_WIDTH), lambda b, i: (0, 0)), vec, vec, vec],
        out_specs=pl.BlockSpec((ts, CV_WIDTH), lambda b, i: (row(b, i), 0)),
        out_shape=jax.ShapeDtypeStruct((bsz * seq, CV_WIDTH), BF16),
        compiler_params=_params("parallel", "parallel"),
        name="conformer_conv",
    )(p_main, p_main, p_main, p_main, p_main, dw_w,
      dw_b.reshape(1, CV_WIDTH), ln_g.reshape(1, CV_WIDTH), ln_b.reshape(1, CV_WIDTH))


def _mem_kernel(q_ref, z_ref, k_ref, v_ref, qg_ref, kg_ref, o_ref):
    for h in range(MEM_HEADS):
        sl = slice(h * MEM_HD, (h + 1) * MEM_HD)
        q = q_ref[:, sl]
        qn = q * lax.rsqrt(jnp.mean(q * q, axis=-1, keepdims=True) + EPS) * qg_ref[...]
        k = k_ref[:, sl]
        kn = k * lax.rsqrt(jnp.mean(k * k, axis=-1, keepdims=True) + EPS) * kg_ref[...]
        s = _mm_nt(qn * (MEM_HD ** -0.5 * LOG2E), kn)
        p = jnp.exp2(s - jnp.max(s, axis=-1, keepdims=True))
        o = _mm(p, v_ref[:, sl]) * (1.0 / jnp.sum(p, axis=-1, keepdims=True))
        o_ref[:, sl] = (o * _silu(z_ref[:, sl])).astype(o_ref.dtype)


def _mem_attend(p, cb_q, cb_z, mkv, q_norm_g, k_norm_g, bsz, seq, ts=512):
    ts = min(ts, seq)
    ns = seq // ts
    mlen = mkv.shape[0] // bsz
    vec = pl.BlockSpec((1, MEM_HD), lambda b, i: (0, 0))
    return pl.pallas_call(
        _mem_kernel,
        grid=(bsz, ns),
        in_specs=[pl.BlockSpec((ts, MEM_WIDTH), lambda b, i: (b * ns + i, cb_q)),
                  pl.BlockSpec((ts, MEM_WIDTH), lambda b, i: (b * ns + i, cb_z)),
                  pl.BlockSpec((mlen, MEM_WIDTH), lambda b, i: (b, 0)),
                  pl.BlockSpec((mlen, MEM_WIDTH), lambda b, i: (b, 1)),
                  vec, vec],
        out_specs=pl.BlockSpec((ts, MEM_WIDTH), lambda b, i: (b * ns + i, 0)),
        out_shape=jax.ShapeDtypeStruct((bsz * seq, MEM_WIDTH), BF16),
        compiler_params=_params("parallel", "parallel"),
        name="mem_attend",
    )(p, p, mkv, mkv, q_norm_g.reshape(1, MEM_HD), k_norm_g.reshape(1, MEM_HD))


def _sb_kernel(q_ref, k_ref, v_ref, z_ref, qg_ref, kg_ref, o_ref, kn_ref, vb_ref, *, nq):
    i = pl.program_id(2)
    tk = SB_TK
    tq = q_ref.shape[0]

    @pl.when(i == 0)
    def _():
        k = k_ref[...]
        kn = k * lax.rsqrt(jnp.mean(k * k, axis=-1, keepdims=True) + EPS) * kg_ref[...]
        kn_ref[...] = kn.astype(BF16)
        vb_ref[...] = v_ref[...].astype(BF16)

    q = q_ref[...]
    qn = q * lax.rsqrt(jnp.mean(q * q, axis=-1, keepdims=True) + EPS) * qg_ref[...]
    qn = (qn * (SB_HD ** -0.5 * LOG2E)).astype(BF16)
    nsub = tq // tk

    uj = lax.broadcasted_iota(jnp.int32, (2 * tk, tk + LANES), 0) & (tk - 1)
    us = lax.broadcasted_iota(jnp.int32, (2 * tk, tk + LANES), 1)
    u_aug = jnp.where((uj >= us) | (us >= tk), 1.0, 0.0).astype(BF16)

    sign_bit = jnp.uint32(0x80000000)

    def sub_block(z, vis):
        neg_abs = lax.bitcast_convert_type(lax.bitcast_convert_type(z, jnp.uint32) | sign_bit, F32)
        cost = jnp.maximum(z, 0.0) + jnp.log2(1.0 + jnp.exp2(neg_abs))
        if vis is not None:
            cost = jnp.where(vis, cost, 0.0)
        hi = cost.astype(BF16)
        lo = (cost - hi.astype(F32)).astype(BF16)
        inc = jnp.dot(jnp.concatenate([hi, lo], axis=1), u_aug, preferred_element_type=F32)
        return z - inc[:, :tk], inc[:, tk:]

    def key_block(key_start, run, diagonal):
        kb = kn_ref[pl.ds(key_start, tq), :]
        vb = vb_ref[pl.ds(key_start, tq), :]
        z_all = lax.dot_general(qn, kb, (((1,), (1,)), ((), ())), preferred_element_type=F32)
        ws = [None] * nsub
        for s in range(nsub - 1, -1, -1):
            vis = None
            if diagonal:
                vis = (lax.broadcasted_iota(jnp.int32, (tq, tk), 1) + s * tk
                       < lax.broadcasted_iota(jnp.int32, (tq, tk), 0))
            pre, tot = sub_block(z_all[:, s * tk:(s + 1) * tk], vis)
            w = jnp.exp2(pre + run)
            if diagonal:
                w = jnp.where(vis, w, 0.0)
            ws[s] = w.astype(BF16)
            run = run - tot
        return jnp.dot(jnp.concatenate(ws, axis=1), vb, preferred_element_type=F32), run

    def query_block(qi):
        run = jnp.zeros((tq, LANES), F32)
        acc = None
        for kj in range(qi, -1, -1):
            pv, run = key_block(kj * tq, run, kj == qi)
            acc = pv if acc is None else acc + pv
        o_ref[...] = (acc * _silu(z_ref[...])).astype(o_ref.dtype)

    for qi in range(nq):
        pl.when(i == qi)(functools.partial(query_block, qi))


def _stickbreak(p, q_norm_g, k_norm_g, bsz, seq):
    tq = min(SB_TQ, seq)
    nq = seq // tq
    nh = SB_HEADS
    vec = pl.BlockSpec((1, SB_HD), lambda b, h, i: (0, 0))
    return pl.pallas_call(
        functools.partial(_sb_kernel, nq=nq),
        grid=(bsz, nh, nq),
        in_specs=[pl.BlockSpec((tq, SB_HD), lambda b, h, i: (b * nq + i, h)),
                  pl.BlockSpec((seq, SB_HD), lambda b, h, i: (b, nh + h)),
                  pl.BlockSpec((seq, SB_HD), lambda b, h, i: (b, 2 * nh + h)),
                  pl.BlockSpec((tq, SB_HD), lambda b, h, i: (b * nq + i, 3 * nh + h)),
                  vec, vec],
        out_specs=pl.BlockSpec((tq, SB_HD), lambda b, h, i: (b * nq + i, h)),
        out_shape=jax.ShapeDtypeStruct((bsz * seq, SB_WIDTH), BF16),
        scratch_shapes=[pltpu.VMEM((seq, SB_HD), BF16), pltpu.VMEM((seq, SB_HD), BF16)],
        compiler_params=_params("parallel", "parallel", "arbitrary"),
        name="stickbreak",
    )(p, p, p, p, q_norm_g.reshape(1, SB_HD), k_norm_g.reshape(1, SB_HD))


def _even_layer(h, mkv, k_norm_g, j, norm_g, w_in, conv_qkv, a_log, dt_bias, dn_norm_g, dw_w, dw_b,
                ln_g, ln_b, q_norm_m, w_out, bsz, seq):
    d_model = h.shape[1]
    xn = _rmsnorm_bf16(h, norm_g)
    n_main = w_in.shape[2] - 2 * DN_HEADS
    w_in_t = jnp.swapaxes(w_in, 1, 2)
    p_main = _proj([xn], w_in_t, j, [0], n_main, transposed=True,
                   skip_from=4 * DN_WIDTH // 1024, skip=2 * DN_HEADS, tn=1024)
    ba = _proj([xn], w_in_t, j, [0], LANES, col0=4 * DN_WIDTH // LANES, transposed=True, tn=LANES)
    gates, gates_t = _gates(ba, a_log, dt_bias)
    o_a = _deltanet(p_main, gates, gates_t, conv_qkv, dn_norm_g, bsz, seq)
    o_b = _conformer(p_main, 4, 5, 6, dw_w, dw_b, ln_g, ln_b, bsz, seq)
    o_m = _mem_attend(p_main, 7, 8, mkv, q_norm_m, k_norm_g, bsz, seq)
    return _proj([o_a, o_b, o_m], w_out, j, [0, 1, 2], d_model, res=h, tn=512)


def _odd_layer(h, mkv, k_norm_g, j, norm_g, w_in, q_norm_c, k_norm_c, q_norm_m, w_out, bsz, seq):
    d_model = h.shape[1]
    xn = _rmsnorm_bf16(h, norm_g)
    p = _proj([xn], w_in, j, [0], w_in.shape[2], tn=1024)
    o_c = _stickbreak(p, q_norm_c, k_norm_c, bsz, seq)
    o_m = _mem_attend(p, 8, 9, mkv, q_norm_m, k_norm_g, bsz, seq)
    return _proj([o_c, o_m], w_out, j, [0, SB_WIDTH // MEM_WIDTH], d_model, res=h, tn=512)


def kernel(x, mem, mem_norm_g, w_mem_kv, mem_k_norm_g, ev_norm_g, ev_w_in, ev_conv_qkv, ev_a_log,
           ev_dt_bias, ev_dn_norm_g, ev_dw_w, ev_dw_b, ev_ln_g, ev_ln_b, ev_q_norm_m, ev_w_out,
           od_norm_g, od_w_in, od_q_norm_c, od_k_norm_c, od_q_norm_m, od_w_out):
    bsz, seq, d = x.shape
    mlen = mem.shape[1]
    depth = ev_norm_g.shape[0] + od_norm_g.shape[0]
    mkv = _proj([_rmsnorm_bf16(mem.reshape(bsz * mlen, d), mem_norm_g, tm=256)],
                w_mem_kv[None], 0, [0], w_mem_kv.shape[1], tm=bsz * mlen, tn=1024)
    h = x.reshape(bsz * seq, d)
    for layer in range(depth):
        j = layer // 2
        if layer % 2 == 0:
            h = _even_layer(h, mkv, mem_k_norm_g, j, ev_norm_g[j], ev_w_in, ev_conv_qkv[j],
                            ev_a_log[j], ev_dt_bias[j], ev_dn_norm_g[j], ev_dw_w[j], ev_dw_b[j],
                            ev_ln_g[j], ev_ln_b[j], ev_q_norm_m[j], ev_w_out, bsz, seq)
        else:
            h = _odd_layer(h, mkv, mem_k_norm_g, j, od_norm_g[j], od_w_in, od_q_norm_c[j],
                           od_k_norm_c[j], od_q_norm_m[j], od_w_out, bsz, seq)
    return h.reshape(bsz, seq, d)
```

```python
import functools

import jax
import jax.numpy as jnp
from jax import lax
from jax.experimental import pallas as pl
from jax.experimental.pallas import tpu as pltpu

F32 = jnp.float32
BF16 = jnp.bfloat16
EPS = 1e-6
LOG2E = 1.4426950408889634

LANES = 128
SUBLANES = 8
VMEM_LIMIT = 48 * 1024 * 1024

DN_HEADS = 8
DN_HD = 128
DN_WIDTH = DN_HEADS * DN_HD
DN_CONV = 4
DN_CHUNK = 128
DN_SPLIT_MAX_POWER = 8
CV_WIDTH = 1024
CV_KERNEL = 31
CV_HALO = 32
SB_HEADS = 16
SB_HD = 128
SB_WIDTH = SB_HEADS * SB_HD
SB_HEADS_PER_STEP = 2
SB_TQ = 512
SB_TK = 128
MEM_HEADS = 4
MEM_HD = 256
MEM_WIDTH = MEM_HEADS * MEM_HD


def _params(*sem):
    return pltpu.CompilerParams(dimension_semantics=sem, vmem_limit_bytes=VMEM_LIMIT)


def _sigmoid(x):
    return 1.0 / (1.0 + jnp.exp(-x))


def _silu(x):
    return x * _sigmoid(x)


def _softplus(x):
    return jnp.maximum(x, 0.0) + jnp.log(1.0 + jnp.exp(-jnp.abs(x)))


def _mm(a, b):
    return jnp.dot(a.astype(BF16), b.astype(BF16), preferred_element_type=F32)


def _mm_nt(a, b):
    return lax.dot_general(a.astype(BF16), b.astype(BF16), (((1,), (1,)), ((), ())),
                           preferred_element_type=F32)


def _bmm(a, b):
    return lax.dot_general(a.astype(BF16), b.astype(BF16), (((2,), (1,)), ((0,), (0,))),
                           preferred_element_type=F32)


def _bmm3(a, b):
    a_hi = a.astype(BF16)
    b_hi = b.astype(BF16)
    a_lo = (a - a_hi.astype(F32)).astype(BF16)
    b_lo = (b - b_hi.astype(F32)).astype(BF16)
    lhs = jnp.concatenate([a_hi, a_hi, a_lo], axis=2)
    rhs = jnp.concatenate([b_hi, b_lo, b_hi], axis=1)
    return lax.dot_general(lhs, rhs, (((2,), (1,)), ((0,), (0,))), preferred_element_type=F32)


def _bmm_nt(a, b):
    return lax.dot_general(a.astype(BF16), b.astype(BF16), (((2,), (2,)), ((0,), (0,))),
                           preferred_element_type=F32)


def _rmsnorm_kernel(x_ref, g_ref, o_ref):
    x = x_ref[...]
    ms = jnp.mean(x * x, axis=-1, keepdims=True)
    o_ref[...] = (x * lax.rsqrt(ms + EPS) * g_ref[...]).astype(o_ref.dtype)


def _rmsnorm_bf16(x2d, g, tm=512):
    m, d = x2d.shape
    return pl.pallas_call(
        _rmsnorm_kernel,
        grid=(m // tm,),
        in_specs=[pl.BlockSpec((tm, d), lambda i: (i, 0)),
                  pl.BlockSpec((1, d), lambda i: (0, 0))],
        out_specs=pl.BlockSpec((tm, d), lambda i: (i, 0)),
        out_shape=jax.ShapeDtypeStruct((m, d), BF16),
        compiler_params=_params("parallel"),
        name="rmsnorm",
    )(x2d, g.reshape(1, d))


PROJ_CAST_ROWS = 256


def _cast_rows(src, dst, offset, tail):
    nrows = src.shape[0]
    rows = min(PROJ_CAST_ROWS, nrows)
    nchunks = nrows // rows

    def body(r, carry):
        start = r * rows
        dst[pl.ds(pl.multiple_of(start, rows), rows), :] = (
            src[pl.ds(pl.multiple_of(start + offset, SUBLANES), rows), :].astype(BF16))
        return carry

    lax.fori_loop(0, nchunks - 1 if offset else nchunks, body, 0)
    if offset:
        last = (nchunks - 1) * rows
        dst[last:nrows - offset, :] = src[last + offset:, :].astype(BF16)
        dst[nrows - offset:, :] = tail[...].astype(BF16)


def _proj_kernel(*refs, transposed, skip_from, skip):
    j = pl.program_id(0)
    i = pl.program_id(1)
    x_ref, w_ref = refs[:2]
    tail_ref = refs[2] if skip_from is not None else None
    o_ref, wb_ref = refs[-2:]

    if skip_from is None:
        pl.when(i == 0)(lambda: _cast_rows(w_ref, wb_ref, 0, None))
    else:
        pl.when((i == 0) & (j < skip_from))(lambda: _cast_rows(w_ref, wb_ref, 0, None))
        pl.when((i == 0) & (j >= skip_from))(lambda: _cast_rows(w_ref, wb_ref, skip, tail_ref))

    contract = (((1,), (1,)), ((), ())) if transposed else (((1,), (0,)), ((), ()))
    o_ref[...] = lax.dot_general(x_ref[...], wb_ref[...], contract, preferred_element_type=F32)


def _proj(x, w, layer, n, col0=0, transposed=False, skip_from=None, skip=0, tm=1024, tn=1024):
    m, k = x.shape
    tm = min(tm, m)
    tn = min(tn, n)
    in_specs = [pl.BlockSpec((tm, k), lambda j, i: (i, 0))]
    if transposed:
        in_specs.append(pl.BlockSpec((None, tn, k), lambda j, i: (layer, col0 + j, 0)))
        scratch = [pltpu.VMEM((tn, k), BF16)]
    else:
        in_specs.append(pl.BlockSpec((None, k, tn), lambda j, i: (layer, 0, col0 + j)))
        scratch = [pltpu.VMEM((k, tn), BF16)]
    args = [x, w]
    if skip_from is not None:
        in_specs.append(pl.BlockSpec((None, skip, k),
                                     lambda j, i: (layer, (col0 + j + 1) * (tn // skip), 0)))
        args.append(w)
    return pl.pallas_call(
        functools.partial(_proj_kernel, transposed=transposed, skip_from=skip_from, skip=skip),
        grid=(n // tn, m // tm),
        in_specs=in_specs,
        out_specs=pl.BlockSpec((tm, tn), lambda j, i: (i, j)),
        out_shape=jax.ShapeDtypeStruct((m, n), F32),
        scratch_shapes=scratch,
        compiler_params=_params("parallel", "arbitrary"),
        name="proj",
    )(*args)


def _out_proj_kernel(*refs, n_pairs, emit_norm, nj):
    i = pl.program_id(0)
    j = pl.program_id(1)
    xs = refs[:n_pairs]
    ws = refs[n_pairs:2 * n_pairs]
    res_ref = refs[2 * n_pairs]
    pos = 2 * n_pairs + 1
    g_ref = refs[pos] if emit_norm else None
    pos += int(emit_norm)
    o_ref = refs[pos]
    xn_ref = refs[pos + 1] if emit_norm else None
    pos += 1 + int(emit_norm)
    wb = refs[pos:pos + n_pairs]
    rows_ref = refs[pos + n_pairs] if emit_norm else None

    @pl.when(i == 0)
    def _():
        for p in range(n_pairs):
            _cast_rows(ws[p], wb[p].at[j], 0, None)

    acc = None
    for p in range(n_pairs):
        d = jnp.dot(xs[p][...], wb[p][j], preferred_element_type=F32)
        acc = d if acc is None else acc + d
    acc = res_ref[...] + acc
    o_ref[...] = acc
    if emit_norm:
        rows_ref[j] = acc

        @pl.when(j == nj - 1)
        def _():
            tn = acc.shape[1]
            ssq = None
            for jj in range(nj):
                t = rows_ref[jj]
                s = jnp.sum(t * t, axis=-1, keepdims=True)
                ssq = s if ssq is None else ssq + s
            scale = lax.rsqrt(ssq * (1.0 / (nj * tn)) + EPS)
            for jj in range(nj):
                sl = slice(jj * tn, (jj + 1) * tn)
                xn_ref[:, sl] = (rows_ref[jj] * scale * g_ref[:, sl]).astype(xn_ref.dtype)


def _out_proj(xs, w, layer, row_blocks, res, next_norm_g=None, tm=512, tn=512):
    m, n = res.shape
    nj = n // tn
    emit_norm = next_norm_g is not None
    in_specs = [pl.BlockSpec((tm, x.shape[1]), lambda i, j: (i, 0)) for x in xs]
    in_specs += [pl.BlockSpec((None, x.shape[1], tn),
                              lambda i, j, rb=rb: (layer, rb, jnp.where(i == 0, j, nj - 1)))
                 for x, rb in zip(xs, row_blocks)]
    in_specs.append(pl.BlockSpec((tm, tn), lambda i, j: (i, j)))
    args = list(xs) + [w] * len(xs) + [res]
    out_specs = [pl.BlockSpec((tm, tn), lambda i, j: (i, j))]
    out_shape = [jax.ShapeDtypeStruct((m, n), F32)]
    scratch = [pltpu.VMEM((nj, x.shape[1], tn), BF16) for x in xs]
    if emit_norm:
        in_specs.append(pl.BlockSpec((1, n), lambda i, j: (0, 0)))
        args.append(next_norm_g.reshape(1, n))
        out_specs.append(pl.BlockSpec((tm, n), lambda i, j: (i, 0)))
        out_shape.append(jax.ShapeDtypeStruct((m, n), BF16))
        scratch.append(pltpu.VMEM((nj, tm, tn), F32))
    outs = pl.pallas_call(
        functools.partial(_out_proj_kernel, n_pairs=len(xs), emit_norm=emit_norm, nj=nj),
        grid=(m // tm, nj),
        in_specs=in_specs,
        out_specs=out_specs,
        out_shape=out_shape,
        scratch_shapes=scratch,
        compiler_params=_params("arbitrary", "arbitrary"),
        name="out_proj",
    )(*args)
    return (outs[0], outs[1]) if emit_norm else (outs[0], None)


def _gates_kernel(ba_ref, alog_ref, dt_ref, g_ref, gt_ref):
    x = ba_ref[...]
    col = lax.broadcasted_iota(jnp.int32, x.shape, 1)
    row = lax.broadcasted_iota(jnp.int32, x.shape, 0)
    beta = _sigmoid(x)
    g = -jnp.exp(alog_ref[...]) * _softplus(x + dt_ref[...])
    rin = row & (DN_CHUNK - 1)
    gc = g
    s = 1
    while s < DN_CHUNK:
        gc = gc + jnp.where(rin >= s, pltpu.roll(gc, s, axis=0), 0.0)
        s *= 2
    out = jnp.where(col < DN_HEADS, beta,
                    jnp.where(col < 2 * DN_HEADS, g, pltpu.roll(gc, DN_HEADS, axis=1)))
    g_ref[...] = out
    gt_ref[...] = out.T


def _gates(ba, a_log, dt_bias, ts=512):
    m = ba.shape[0]
    ts = min(ts, m)
    pad = lambda v: jnp.zeros((1, LANES), F32).at[0, DN_HEADS:2 * DN_HEADS].set(v.astype(F32))
    return pl.pallas_call(
        _gates_kernel,
        grid=(m // ts,),
        in_specs=[pl.BlockSpec((ts, LANES), lambda i: (i, 0)),
                  pl.BlockSpec((1, LANES), lambda i: (0, 0)),
                  pl.BlockSpec((1, LANES), lambda i: (0, 0))],
        out_specs=[pl.BlockSpec((ts, LANES), lambda i: (i, 0)),
                   pl.BlockSpec((LANES, ts), lambda i: (0, i))],
        out_shape=[jax.ShapeDtypeStruct((m, LANES), F32),
                   jax.ShapeDtypeStruct((LANES, m), F32)],
        compiler_params=_params("parallel"),
        name="dn_gates",
    )(ba, pad(a_log), pad(dt_bias))


def _dn_kernel(q_ref, k_ref, v_ref, hq_ref, hk_ref, hv_ref, z_ref, g_ref, gt_ref,
               wq_ref, wk_ref, wv_ref, ng_ref, o_ref, state_ref):
    i = pl.program_id(1)
    c = DN_CHUNK

    @pl.when(i == 0)
    def _():
        state_ref[...] = jnp.zeros_like(state_ref)

    def conv_act(cur_ref, halo_ref, w_ref):
        halo = jnp.where(i == 0, 0.0, halo_ref[...])
        x = jnp.concatenate([halo, cur_ref[...]], axis=0)
        w = w_ref[...]
        y = x * w[DN_CONV - 1:DN_CONV, :]
        for tap in range(DN_CONV - 1):
            y = y + pltpu.roll(x, DN_CONV - 1 - tap, axis=0) * w[tap:tap + 1, :]
        return _silu(y[SUBLANES:, :])

    heads = lambda t: jnp.stack([t[:, h * DN_HD:(h + 1) * DN_HD] for h in range(DN_HEADS)])
    q = heads(conv_act(q_ref, hq_ref, wq_ref))
    k = heads(conv_act(k_ref, hk_ref, wk_ref))
    v = heads(conv_act(v_ref, hv_ref, wv_ref))
    q = q * lax.rsqrt(jnp.sum(q * q, axis=-1, keepdims=True) + EPS) * (DN_HD ** -0.5)
    k = k * lax.rsqrt(jnp.sum(k * k, axis=-1, keepdims=True) + EPS)
    gates = g_ref[...]
    gates_t = gt_ref[...]
    beta = jnp.stack([gates[:, h:h + 1] for h in range(DN_HEADS)])
    gcol = jnp.stack([gates[:, 2 * DN_HEADS + h:2 * DN_HEADS + h + 1] for h in range(DN_HEADS)])
    grow = jnp.stack([gates_t[2 * DN_HEADS + h:2 * DN_HEADS + h + 1, :] for h in range(DN_HEADS)])
    glast = grow[:, :, c - 1:c]

    ri = lax.broadcasted_iota(jnp.int32, (DN_HEADS, c, c), 1)
    ci = lax.broadcasted_iota(jnp.int32, (DN_HEADS, c, c), 2)
    decay = jnp.exp(jnp.minimum(gcol - grow, 0.0))
    kb = k * beta
    s2 = _bmm_nt(jnp.concatenate([kb, q], axis=1), k)
    a_mat = jnp.where(ri > ci, s2[:, :c] * decay, 0.0)
    qk = jnp.where(ri >= ci, s2[:, c:] * decay, 0.0)
    x = _bmm3(a_mat, a_mat)
    n = -a_mat
    p = 2
    while 2 * p < c:
        mm = _bmm3 if p <= DN_SPLIT_MAX_POWER else _bmm
        both = mm(x, jnp.concatenate([x, n], axis=-1))
        n = n + x + both[:, :, c:]
        x = both[:, :, :c]
        p *= 2
    n = n + x + _bmm(x, n)
    egc = jnp.exp(gcol)
    rhs = jnp.concatenate([v * beta, kb * egc], axis=-1)
    sol = rhs + _bmm(n, rhs)
    u_val, w_key = sol[:, :, :DN_HD], sol[:, :, DN_HD:]
    state = state_ref[...]
    both = _bmm(jnp.concatenate([w_key, q * egc], axis=1), state)
    v_new = u_val - both[:, :c]
    kd_t = jnp.swapaxes(k * jnp.exp(glast - gcol), 1, 2)
    tail = _bmm(jnp.concatenate([qk, kd_t], axis=1), v_new)
    o = both[:, c:] + tail[:, :c]
    state_ref[...] = state * jnp.exp(glast) + tail[:, c:]
    o = o * lax.rsqrt(jnp.mean(o * o, axis=-1, keepdims=True) + EPS) * ng_ref[...]
    z_all = z_ref[...]
    for h in range(DN_HEADS):
        sl = slice(h * DN_HD, (h + 1) * DN_HD)
        o_ref[:, sl] = (o[h] * _silu(z_all[:, sl])).astype(o_ref.dtype)


def _deltanet(p_main, gates, gates_t, conv_w, norm_g, bsz, seq):
    c = DN_CHUNK
    ns = seq // c
    row = lambda b, i: b * ns + i
    halo_row = lambda b, i: jnp.maximum((b * ns + i) * (c // SUBLANES) - 1, 0)
    cur = lambda cb: pl.BlockSpec((c, DN_WIDTH), lambda b, i: (row(b, i), cb))
    halo = lambda cb: pl.BlockSpec((SUBLANES, DN_WIDTH), lambda b, i: (halo_row(b, i), cb))
    wspec = lambda cb: pl.BlockSpec((DN_CONV, DN_WIDTH), lambda b, i: (0, cb))
    return pl.pallas_call(
        _dn_kernel,
        grid=(bsz, ns),
        in_specs=[cur(0), cur(1), cur(2), halo(0), halo(1), halo(2), cur(3),
                  pl.BlockSpec((c, LANES), lambda b, i: (row(b, i), 0)),
                  pl.BlockSpec((LANES, c), lambda b, i: (0, row(b, i))),
                  wspec(0), wspec(1), wspec(2),
                  pl.BlockSpec((1, DN_HD), lambda b, i: (0, 0))],
        out_specs=pl.BlockSpec((c, DN_WIDTH), lambda b, i: (row(b, i), 0)),
        out_shape=jax.ShapeDtypeStruct((bsz * seq, DN_WIDTH), BF16),
        scratch_shapes=[pltpu.VMEM((DN_HEADS, DN_HD, DN_HD), F32)],
        compiler_params=_params("parallel", "arbitrary"),
        name="deltanet",
    )(p_main, p_main, p_main, p_main, p_main, p_main, p_main, gates, gates_t,
      conv_w, conv_w, conv_w, norm_g.reshape(1, DN_HD))


def _cv_kernel(a_ref, b_ref, ha_ref, hb_ref, z_ref, w_ref, bias_ref, lg_ref, lb_ref, o_ref):
    i = pl.program_id(1)
    ts = a_ref.shape[0]
    glu_cur = a_ref[...] * _sigmoid(b_ref[...])
    glu_halo = jnp.where(i == 0, 0.0, ha_ref[...] * _sigmoid(hb_ref[...]))
    x = jnp.concatenate([glu_halo, glu_cur], axis=0)
    w = w_ref[...]
    shifted = [x] + [pltpu.roll(x, r, axis=0) for r in range(1, SUBLANES)]
    acc = None
    for tap in range(CV_KERNEL):
        back = CV_KERNEL - 1 - tap
        start = CV_HALO - (back // SUBLANES) * SUBLANES
        term = shifted[back % SUBLANES][start:start + ts, :] * w[tap:tap + 1, :]
        acc = term if acc is None else acc + term
    y = acc + bias_ref[...]
    mu = jnp.mean(y, axis=-1, keepdims=True)
    yc = y - mu
    y = yc * lax.rsqrt(jnp.mean(yc * yc, axis=-1, keepdims=True) + EPS) * lg_ref[...] + lb_ref[...]
    o_ref[...] = (_silu(y) * _silu(z_ref[...])).astype(o_ref.dtype)


def _conformer(p_main, cb_a, cb_b, cb_z, dw_w, dw_b, ln_g, ln_b, bsz, seq, ts=256):
    ts = min(ts, seq)
    ns = seq // ts
    row = lambda b, i: b * ns + i
    halo_row = lambda b, i: jnp.maximum((b * ns + i) * (ts // CV_HALO) - 1, 0)
    cur = lambda cb: pl.BlockSpec((ts, CV_WIDTH), lambda b, i: (row(b, i), cb))
    halo = lambda cb: pl.BlockSpec((CV_HALO, CV_WIDTH), lambda b, i: (halo_row(b, i), cb))
    vec = pl.BlockSpec((1, CV_WIDTH), lambda b, i: (0, 0))
    return pl.pallas_call(
        _cv_kernel,
        grid=(bsz, ns),
        in_specs=[cur(cb_a), cur(cb_b), halo(cb_a), halo(cb_b), cur(cb_z),
                  pl.BlockSpec((CV_KERNEL, CV_WIDTH), lambda b, i: (0, 0)), vec, vec, vec],
        out_specs=pl.BlockSpec((ts, CV_WIDTH), lambda b, i: (row(b, i), 0)),
        out_shape=jax.ShapeDtypeStruct((bsz * seq, CV_WIDTH), BF16),
        compiler_params=_params("parallel", "parallel"),
        name="conformer_conv",
    )(p_main, p_main, p_main, p_main, p_main, dw_w,
      dw_b.reshape(1, CV_WIDTH), ln_g.reshape(1, CV_WIDTH), ln_b.reshape(1, CV_WIDTH))


def _mem_kernel(q_ref, z_ref, k_ref, v_ref, qg_ref, kg_ref, o_ref):
    for h in range(MEM_HEADS):
        sl = slice(h * MEM_HD, (h + 1) * MEM_HD)
        q = q_ref[:, sl]
        qn = q * lax.rsqrt(jnp.mean(q * q, axis=-1, keepdims=True) + EPS) * qg_ref[...]
        k = k_ref[:, sl]
        kn = k * lax.rsqrt(jnp.mean(k * k, axis=-1, keepdims=True) + EPS) * kg_ref[...]
        s = _mm_nt(qn * (MEM_HD ** -0.5 * LOG2E), kn)
        p = jnp.exp2(s - jnp.max(s, axis=-1, keepdims=True))
        o = _mm(p, v_ref[:, sl]) * (1.0 / jnp.sum(p, axis=-1, keepdims=True))
        o_ref[:, sl] = (o * _silu(z_ref[:, sl])).astype(o_ref.dtype)


def _mem_attend(p, cb_q, cb_z, mkv, q_norm_g, k_norm_g, bsz, seq, ts=512):
    ts = min(ts, seq)
    ns = seq // ts
    mlen = mkv.shape[0] // bsz
    vec = pl.BlockSpec((1, MEM_HD), lambda b, i: (0, 0))
    return pl.pallas_call(
        _mem_kernel,
        grid=(bsz, ns),
        in_specs=[pl.BlockSpec((ts, MEM_WIDTH), lambda b, i: (b * ns + i, cb_q)),
                  pl.BlockSpec((ts, MEM_WIDTH), lambda b, i: (b * ns + i, cb_z)),
                  pl.BlockSpec((mlen, MEM_WIDTH), lambda b, i: (b, 0)),
                  pl.BlockSpec((mlen, MEM_WIDTH), lambda b, i: (b, 1)),
                  vec, vec],
        out_specs=pl.BlockSpec((ts, MEM_WIDTH), lambda b, i: (b * ns + i, 0)),
        out_shape=jax.ShapeDtypeStruct((bsz * seq, MEM_WIDTH), BF16),
        compiler_params=_params("parallel", "parallel"),
        name="mem_attend",
    )(p, p, mkv, mkv, q_norm_g.reshape(1, MEM_HD), k_norm_g.reshape(1, MEM_HD))


def _sb_kernel(q_ref, k_ref, v_ref, z_ref, qg_ref, kg_ref, o_ref, kn_ref, vb_ref, *, nq):
    i = pl.program_id(2)
    tk = SB_TK
    tq = q_ref.shape[0]
    head_lanes = [slice(hh * SB_HD, (hh + 1) * SB_HD) for hh in range(SB_HEADS_PER_STEP)]

    @pl.when(i == 0)
    def _():
        for hl in head_lanes:
            k = k_ref[:, hl]
            kn = k * lax.rsqrt(jnp.mean(k * k, axis=-1, keepdims=True) + EPS) * kg_ref[...]
            kn_ref[:, hl] = kn.astype(BF16)
        vb_ref[...] = v_ref[...].astype(BF16)

    qns = []
    for hl in head_lanes:
        q = q_ref[:, hl]
        qn = q * lax.rsqrt(jnp.mean(q * q, axis=-1, keepdims=True) + EPS) * qg_ref[...]
        qns.append((qn * (SB_HD ** -0.5 * LOG2E)).astype(BF16))
    nsub = tq // tk

    uj = lax.broadcasted_iota(jnp.int32, (2 * tk, tk + LANES), 0) & (tk - 1)
    us = lax.broadcasted_iota(jnp.int32, (2 * tk, tk + LANES), 1)
    u_aug = jnp.where((uj >= us) | (us >= tk), 1.0, 0.0).astype(BF16)

    sign_bit = jnp.uint32(0x80000000)

    def sub_block(z, vis):
        neg_abs = lax.bitcast_convert_type(lax.bitcast_convert_type(z, jnp.uint32) | sign_bit, F32)
        cost = jnp.maximum(z, 0.0) + jnp.log2(1.0 + jnp.exp2(neg_abs))
        if vis is not None:
            cost = jnp.where(vis, cost, 0.0)
        hi = cost.astype(BF16)
        lo = (cost - hi.astype(F32)).astype(BF16)
        inc = jnp.dot(jnp.concatenate([hi, lo], axis=1), u_aug, preferred_element_type=F32)
        return z - inc[:, :tk], inc[:, tk:]

    def key_block(hh, key_start, run, diagonal):
        kb = kn_ref[pl.ds(key_start, tq), head_lanes[hh]]
        vb = vb_ref[pl.ds(key_start, tq), head_lanes[hh]]
        z_all = lax.dot_general(qns[hh], kb, (((1,), (1,)), ((), ())),
                                preferred_element_type=F32)
        ws = [None] * nsub
        for s in range(nsub - 1, -1, -1):
            vis = None
            if diagonal:
                vis = (lax.broadcasted_iota(jnp.int32, (tq, tk), 1) + s * tk
                       < lax.broadcasted_iota(jnp.int32, (tq, tk), 0))
            pre, tot = sub_block(z_all[:, s * tk:(s + 1) * tk], vis)
            w = jnp.exp2(pre + run)
            if diagonal:
                w = jnp.where(vis, w, 0.0)
            ws[s] = w.astype(BF16)
            run = run - tot
        return jnp.dot(jnp.concatenate(ws, axis=1), vb, preferred_element_type=F32), run

    def query_block(qi):
        runs = [jnp.zeros((tq, LANES), F32) for _ in head_lanes]
        accs = [None for _ in head_lanes]
        for kj in range(qi, -1, -1):
            for hh in range(len(head_lanes)):
                pv, runs[hh] = key_block(hh, kj * tq, runs[hh], kj == qi)
                accs[hh] = pv if accs[hh] is None else accs[hh] + pv
        for hh, hl in enumerate(head_lanes):
            o_ref[:, hl] = (accs[hh] * _silu(z_ref[:, hl])).astype(o_ref.dtype)

    for qi in range(nq):
        pl.when(i == qi)(functools.partial(query_block, qi))


def _stickbreak(p, q_norm_g, k_norm_g, bsz, seq):
    tq = min(SB_TQ, seq)
    nq = seq // tq
    ng = SB_HEADS // SB_HEADS_PER_STEP
    gw = SB_HEADS_PER_STEP * SB_HD
    vec = pl.BlockSpec((1, SB_HD), lambda b, h, i: (0, 0))
    return pl.pallas_call(
        functools.partial(_sb_kernel, nq=nq),
        grid=(bsz, ng, nq),
        in_specs=[pl.BlockSpec((tq, gw), lambda b, h, i: (b * nq + i, h)),
                  pl.BlockSpec((seq, gw), lambda b, h, i: (b, ng + h)),
                  pl.BlockSpec((seq, gw), lambda b, h, i: (b, 2 * ng + h)),
                  pl.BlockSpec((tq, gw), lambda b, h, i: (b * nq + i, 3 * ng + h)),
                  vec, vec],
        out_specs=pl.BlockSpec((tq, gw), lambda b, h, i: (b * nq + i, h)),
        out_shape=jax.ShapeDtypeStruct((bsz * seq, SB_WIDTH), BF16),
        scratch_shapes=[pltpu.VMEM((seq, gw), BF16), pltpu.VMEM((seq, gw), BF16)],
        compiler_params=_params("parallel", "parallel", "arbitrary"),
        name="stickbreak",
    )(p, p, p, p, q_norm_g.reshape(1, SB_HD), k_norm_g.reshape(1, SB_HD))


def _even_layer(h, xn, mkv, k_norm_g, j, w_in, conv_qkv, a_log, dt_bias, dn_norm_g, dw_w, dw_b,
                ln_g, ln_b, q_norm_m, w_out, next_norm_g, bsz, seq):
    n_main = w_in.shape[2] - 2 * DN_HEADS
    w_in_t = jnp.swapaxes(w_in, 1, 2)
    p_main = _proj(xn, w_in_t, j, n_main, transposed=True,
                   skip_from=4 * DN_WIDTH // 1024, skip=2 * DN_HEADS, tn=1024)
    ba = _proj(xn, w_in_t, j, LANES, col0=4 * DN_WIDTH // LANES, transposed=True, tn=LANES)
    gates, gates_t = _gates(ba, a_log, dt_bias)
    o_a = _deltanet(p_main, gates, gates_t, conv_qkv, dn_norm_g, bsz, seq)
    o_b = _conformer(p_main, 4, 5, 6, dw_w, dw_b, ln_g, ln_b, bsz, seq)
    o_m = _mem_attend(p_main, 7, 8, mkv, q_norm_m, k_norm_g, bsz, seq)
    return _out_proj([o_a, o_b, o_m], w_out, j, [0, 1, 2], h, next_norm_g)


def _odd_layer(h, xn, mkv, k_norm_g, j, w_in, q_norm_c, k_norm_c, q_norm_m, w_out, next_norm_g,
               bsz, seq):
    p = _proj(xn, w_in, j, w_in.shape[2])
    o_c = _stickbreak(p, q_norm_c, k_norm_c, bsz, seq)
    o_m = _mem_attend(p, 8, 9, mkv, q_norm_m, k_norm_g, bsz, seq)
    return _out_proj([o_c, o_m], w_out, j, [0, SB_WIDTH // MEM_WIDTH], h, next_norm_g)


def kernel(x, mem, mem_norm_g, w_mem_kv, mem_k_norm_g, ev_norm_g, ev_w_in, ev_conv_qkv, ev_a_log,
           ev_dt_bias, ev_dn_norm_g, ev_dw_w, ev_dw_b, ev_ln_g, ev_ln_b, ev_q_norm_m, ev_w_out,
           od_norm_g, od_w_in, od_q_norm_c, od_k_norm_c, od_q_norm_m, od_w_out):
    bsz, seq, d = x.shape
    mlen = mem.shape[1]
    depth = ev_norm_g.shape[0] + od_norm_g.shape[0]
    mkv = _proj(_rmsnorm_bf16(mem.reshape(bsz * mlen, d), mem_norm_g, tm=256),
                w_mem_kv[None], 0, w_mem_kv.shape[1])
    h = x.reshape(bsz * seq, d)
    layer_gain = lambda layer: (ev_norm_g if layer % 2 == 0 else od_norm_g)[layer // 2]
    xn = _rmsnorm_bf16(h, layer_gain(0))
    for layer in range(depth):
        j = layer // 2
        next_g = layer_gain(layer + 1) if layer + 1 < depth else None
        if layer % 2 == 0:
            h, xn = _even_layer(h, xn, mkv, mem_k_norm_g, j, ev_w_in, ev_conv_qkv[j], ev_a_log[j],
                                ev_dt_bias[j], ev_dn_norm_g[j], ev_dw_w[j], ev_dw_b[j], ev_ln_g[j],
                                ev_ln_b[j], ev_q_norm_m[j], ev_w_out, next_g, bsz, seq)
        else:
            h, xn = _odd_layer(h, xn, mkv, mem_k_norm_g, j, od_w_in, od_q_norm_c[j], od_k_norm_c[j],
                               od_q_norm_m[j], od_w_out, next_g, bsz, seq)
    return h.reshape(bsz, seq, d)
```

```python
import functools

import jax
import jax.numpy as jnp
from jax import lax
from jax.experimental import pallas as pl
from jax.experimental.pallas import tpu as pltpu

F32 = jnp.float32
BF16 = jnp.bfloat16
EPS = 1e-6
LOG2E = 1.4426950408889634

LANES = 128
SUBLANES = 8
VMEM_LIMIT = 48 * 1024 * 1024

DN_HEADS = 8
DN_HD = 128
DN_WIDTH = DN_HEADS * DN_HD
DN_CONV = 4
DN_CHUNK = 128
DN_CHUNKS_PER_STEP = 2
DN_SPLIT_MAX_POWER = 8
CV_WIDTH = 1024
CV_KERNEL = 31
CV_HALO = 32
SB_HEADS = 16
SB_HD = 128
SB_WIDTH = SB_HEADS * SB_HD
SB_HEADS_PER_STEP = 2
SB_TQ = 512
SB_TK = 128
MEM_HEADS = 4
MEM_HD = 256
MEM_WIDTH = MEM_HEADS * MEM_HD


def _params(*sem):
    return pltpu.CompilerParams(dimension_semantics=sem, vmem_limit_bytes=VMEM_LIMIT)


def _sigmoid(x):
    return 1.0 / (1.0 + jnp.exp(-x))


def _silu(x):
    return x * _sigmoid(x)


def _softplus(x):
    return jnp.maximum(x, 0.0) + jnp.log(1.0 + jnp.exp(-jnp.abs(x)))


def _mm(a, b):
    return jnp.dot(a.astype(BF16), b.astype(BF16), preferred_element_type=F32)


def _mm_nt(a, b):
    return lax.dot_general(a.astype(BF16), b.astype(BF16), (((1,), (1,)), ((), ())),
                           preferred_element_type=F32)


def _bmm(a, b):
    return lax.dot_general(a.astype(BF16), b.astype(BF16), (((2,), (1,)), ((0,), (0,))),
                           preferred_element_type=F32)


def _bmm3(a, b):
    a_hi = a.astype(BF16)
    b_hi = b.astype(BF16)
    a_lo = (a - a_hi.astype(F32)).astype(BF16)
    b_lo = (b - b_hi.astype(F32)).astype(BF16)
    lhs = jnp.concatenate([a_hi, a_hi, a_lo], axis=2)
    rhs = jnp.concatenate([b_hi, b_lo, b_hi], axis=1)
    return lax.dot_general(lhs, rhs, (((2,), (1,)), ((0,), (0,))), preferred_element_type=F32)


def _bmm_nt(a, b):
    return lax.dot_general(a.astype(BF16), b.astype(BF16), (((2,), (2,)), ((0,), (0,))),
                           preferred_element_type=F32)


def _rmsnorm_kernel(x_ref, g_ref, o_ref):
    x = x_ref[...]
    ms = jnp.mean(x * x, axis=-1, keepdims=True)
    o_ref[...] = (x * lax.rsqrt(ms + EPS) * g_ref[...]).astype(o_ref.dtype)


def _rmsnorm_bf16(x2d, g, tm=512):
    m, d = x2d.shape
    return pl.pallas_call(
        _rmsnorm_kernel,
        grid=(m // tm,),
        in_specs=[pl.BlockSpec((tm, d), lambda i: (i, 0)),
                  pl.BlockSpec((1, d), lambda i: (0, 0))],
        out_specs=pl.BlockSpec((tm, d), lambda i: (i, 0)),
        out_shape=jax.ShapeDtypeStruct((m, d), BF16),
        compiler_params=_params("parallel"),
        name="rmsnorm",
    )(x2d, g.reshape(1, d))


PROJ_CAST_ROWS = 256


def _cast_rows(src, dst, offset, tail):
    nrows = src.shape[0]
    rows = min(PROJ_CAST_ROWS, nrows)
    nchunks = nrows // rows

    def body(r, carry):
        start = r * rows
        dst[pl.ds(pl.multiple_of(start, rows), rows), :] = (
            src[pl.ds(pl.multiple_of(start + offset, SUBLANES), rows), :].astype(BF16))
        return carry

    lax.fori_loop(0, nchunks - 1 if offset else nchunks, body, 0)
    if offset:
        last = (nchunks - 1) * rows
        dst[last:nrows - offset, :] = src[last + offset:, :].astype(BF16)
        dst[nrows - offset:, :] = tail[...].astype(BF16)


def _proj_kernel(*refs, transposed, skip_from, skip):
    j = pl.program_id(0)
    i = pl.program_id(1)
    x_ref, w_ref = refs[:2]
    tail_ref = refs[2] if skip_from is not None else None
    o_ref, wb_ref = refs[-2:]

    if skip_from is None:
        pl.when(i == 0)(lambda: _cast_rows(w_ref, wb_ref, 0, None))
    else:
        pl.when((i == 0) & (j < skip_from))(lambda: _cast_rows(w_ref, wb_ref, 0, None))
        pl.when((i == 0) & (j >= skip_from))(lambda: _cast_rows(w_ref, wb_ref, skip, tail_ref))

    contract = (((1,), (1,)), ((), ())) if transposed else (((1,), (0,)), ((), ()))
    o_ref[...] = lax.dot_general(x_ref[...], wb_ref[...], contract, preferred_element_type=F32)


def _proj(x, w, layer, n, col0=0, transposed=False, skip_from=None, skip=0, tm=1024, tn=1024):
    m, k = x.shape
    tm = min(tm, m)
    tn = min(tn, n)
    in_specs = [pl.BlockSpec((tm, k), lambda j, i: (i, 0))]
    if transposed:
        in_specs.append(pl.BlockSpec((None, tn, k), lambda j, i: (layer, col0 + j, 0)))
        scratch = [pltpu.VMEM((tn, k), BF16)]
    else:
        in_specs.append(pl.BlockSpec((None, k, tn), lambda j, i: (layer, 0, col0 + j)))
        scratch = [pltpu.VMEM((k, tn), BF16)]
    args = [x, w]
    if skip_from is not None:
        in_specs.append(pl.BlockSpec((None, skip, k),
                                     lambda j, i: (layer, (col0 + j + 1) * (tn // skip), 0)))
        args.append(w)
    return pl.pallas_call(
        functools.partial(_proj_kernel, transposed=transposed, skip_from=skip_from, skip=skip),
        grid=(n // tn, m // tm),
        in_specs=in_specs,
        out_specs=pl.BlockSpec((tm, tn), lambda j, i: (i, j)),
        out_shape=jax.ShapeDtypeStruct((m, n), F32),
        scratch_shapes=scratch,
        compiler_params=_params("parallel", "arbitrary"),
        name="proj",
    )(*args)


def _out_proj_kernel(*refs, n_pairs, emit_norm, nj):
    i = pl.program_id(0)
    j = pl.program_id(1)
    xs = refs[:n_pairs]
    ws = refs[n_pairs:2 * n_pairs]
    res_ref = refs[2 * n_pairs]
    pos = 2 * n_pairs + 1
    g_ref = refs[pos] if emit_norm else None
    pos += int(emit_norm)
    o_ref = refs[pos]
    xn_ref = refs[pos + 1] if emit_norm else None
    pos += 1 + int(emit_norm)
    wb = refs[pos:pos + n_pairs]
    rows_ref = refs[pos + n_pairs] if emit_norm else None

    @pl.when(i == 0)
    def _():
        for p in range(n_pairs):
            _cast_rows(ws[p], wb[p].at[j], 0, None)

    acc = None
    for p in range(n_pairs):
        d = jnp.dot(xs[p][...], wb[p][j], preferred_element_type=F32)
        acc = d if acc is None else acc + d
    acc = res_ref[...] + acc
    o_ref[...] = acc
    if emit_norm:
        rows_ref[j] = acc

        @pl.when(j == nj - 1)
        def _():
            tn = acc.shape[1]
            ssq = None
            for jj in range(nj):
                t = rows_ref[jj]
                s = jnp.sum(t * t, axis=-1, keepdims=True)
                ssq = s if ssq is None else ssq + s
            scale = lax.rsqrt(ssq * (1.0 / (nj * tn)) + EPS)
            for jj in range(nj):
                sl = slice(jj * tn, (jj + 1) * tn)
                xn_ref[:, sl] = (rows_ref[jj] * scale * g_ref[:, sl]).astype(xn_ref.dtype)


def _out_proj(xs, w, layer, row_blocks, res, next_norm_g=None, tm=512, tn=512):
    m, n = res.shape
    nj = n // tn
    emit_norm = next_norm_g is not None
    in_specs = [pl.BlockSpec((tm, x.shape[1]), lambda i, j: (i, 0)) for x in xs]
    in_specs += [pl.BlockSpec((None, x.shape[1], tn),
                              lambda i, j, rb=rb: (layer, rb, jnp.where(i == 0, j, nj - 1)))
                 for x, rb in zip(xs, row_blocks)]
    in_specs.append(pl.BlockSpec((tm, tn), lambda i, j: (i, j)))
    args = list(xs) + [w] * len(xs) + [res]
    out_specs = [pl.BlockSpec((tm, tn), lambda i, j: (i, j))]
    out_shape = [jax.ShapeDtypeStruct((m, n), F32)]
    scratch = [pltpu.VMEM((nj, x.shape[1], tn), BF16) for x in xs]
    if emit_norm:
        in_specs.append(pl.BlockSpec((1, n), lambda i, j: (0, 0)))
        args.append(next_norm_g.reshape(1, n))
        out_specs.append(pl.BlockSpec((tm, n), lambda i, j: (i, 0)))
        out_shape.append(jax.ShapeDtypeStruct((m, n), BF16))
        scratch.append(pltpu.VMEM((nj, tm, tn), F32))
    outs = pl.pallas_call(
        functools.partial(_out_proj_kernel, n_pairs=len(xs), emit_norm=emit_norm, nj=nj),
        grid=(m // tm, nj),
        in_specs=in_specs,
        out_specs=out_specs,
        out_shape=out_shape,
        scratch_shapes=scratch,
        compiler_params=_params("arbitrary", "arbitrary"),
        name="out_proj",
    )(*args)
    return (outs[0], outs[1]) if emit_norm else (outs[0], None)


def _gates_kernel(ba_ref, alog_ref, dt_ref, g_ref, gt_ref):
    x = ba_ref[...]
    col = lax.broadcasted_iota(jnp.int32, x.shape, 1)
    row = lax.broadcasted_iota(jnp.int32, x.shape, 0)
    beta = _sigmoid(x)
    g = -jnp.exp(alog_ref[...]) * _softplus(x + dt_ref[...])
    rin = row & (DN_CHUNK - 1)
    gc = g
    s = 1
    while s < DN_CHUNK:
        gc = gc + jnp.where(rin >= s, pltpu.roll(gc, s, axis=0), 0.0)
        s *= 2
    out = jnp.where(col < DN_HEADS, beta,
                    jnp.where(col < 2 * DN_HEADS, g, pltpu.roll(gc, DN_HEADS, axis=1)))
    g_ref[...] = out
    gt_ref[...] = out.T


def _gates(ba, a_log, dt_bias, ts=512):
    m = ba.shape[0]
    ts = min(ts, m)
    pad = lambda v: jnp.zeros((1, LANES), F32).at[0, DN_HEADS:2 * DN_HEADS].set(v.astype(F32))
    return pl.pallas_call(
        _gates_kernel,
        grid=(m // ts,),
        in_specs=[pl.BlockSpec((ts, LANES), lambda i: (i, 0)),
                  pl.BlockSpec((1, LANES), lambda i: (0, 0)),
                  pl.BlockSpec((1, LANES), lambda i: (0, 0))],
        out_specs=[pl.BlockSpec((ts, LANES), lambda i: (i, 0)),
                   pl.BlockSpec((LANES, ts), lambda i: (0, i))],
        out_shape=[jax.ShapeDtypeStruct((m, LANES), F32),
                   jax.ShapeDtypeStruct((LANES, m), F32)],
        compiler_params=_params("parallel"),
        name="dn_gates",
    )(ba, pad(a_log), pad(dt_bias))


def _dn_kernel(q_ref, k_ref, v_ref, hq_ref, hk_ref, hv_ref, z_ref, g_ref, gt_ref,
               wq_ref, wk_ref, wv_ref, ng_ref, o_ref, state_ref):
    i = pl.program_id(1)
    c = DN_CHUNK

    @pl.when(i == 0)
    def _():
        state_ref[...] = jnp.zeros_like(state_ref)

    def conv_act(cur_ref, halo_ref, w_ref):
        halo = jnp.where(i == 0, 0.0, halo_ref[...])
        x = jnp.concatenate([halo, cur_ref[...]], axis=0)
        w = w_ref[...]
        y = x * w[DN_CONV - 1:DN_CONV, :]
        for tap in range(DN_CONV - 1):
            y = y + pltpu.roll(x, DN_CONV - 1 - tap, axis=0) * w[tap:tap + 1, :]
        return _silu(y[SUBLANES:, :])

    q_all = conv_act(q_ref, hq_ref, wq_ref)
    k_all = conv_act(k_ref, hk_ref, wk_ref)
    v_all = conv_act(v_ref, hv_ref, wv_ref)
    ri = lax.broadcasted_iota(jnp.int32, (DN_HEADS, c, c), 1)
    ci = lax.broadcasted_iota(jnp.int32, (DN_HEADS, c, c), 2)
    heads = lambda t: jnp.stack([t[:, h * DN_HD:(h + 1) * DN_HD] for h in range(DN_HEADS)])

    def chunk(r0, state):
        q = heads(q_all[r0:r0 + c])
        k = heads(k_all[r0:r0 + c])
        v = heads(v_all[r0:r0 + c])
        q = q * lax.rsqrt(jnp.sum(q * q, axis=-1, keepdims=True) + EPS) * (DN_HD ** -0.5)
        k = k * lax.rsqrt(jnp.sum(k * k, axis=-1, keepdims=True) + EPS)
        gates = g_ref[r0:r0 + c, :]
        gates_t = gt_ref[:, r0:r0 + c]
        beta = jnp.stack([gates[:, h:h + 1] for h in range(DN_HEADS)])
        gcol = jnp.stack([gates[:, 2 * DN_HEADS + h:2 * DN_HEADS + h + 1]
                          for h in range(DN_HEADS)])
        grow = jnp.stack([gates_t[2 * DN_HEADS + h:2 * DN_HEADS + h + 1, :]
                          for h in range(DN_HEADS)])
        glast = grow[:, :, c - 1:c]
        decay = jnp.exp(jnp.minimum(gcol - grow, 0.0))
        kb = k * beta
        s2 = _bmm_nt(jnp.concatenate([kb, q], axis=1), k)
        a_mat = jnp.where(ri > ci, s2[:, :c] * decay, 0.0)
        qk = jnp.where(ri >= ci, s2[:, c:] * decay, 0.0)
        x = _bmm3(a_mat, a_mat)
        n = -a_mat
        p = 2
        while 2 * p < c:
            mm = _bmm3 if p <= DN_SPLIT_MAX_POWER else _bmm
            both = mm(x, jnp.concatenate([x, n], axis=-1))
            n = n + x + both[:, :, c:]
            x = both[:, :, :c]
            p *= 2
        n = n + x + _bmm(x, n)
        egc = jnp.exp(gcol)
        rhs = jnp.concatenate([v * beta, kb * egc], axis=-1)
        sol = rhs + _bmm(n, rhs)
        u_val, w_key = sol[:, :, :DN_HD], sol[:, :, DN_HD:]
        both = _bmm(jnp.concatenate([w_key, q * egc], axis=1), state)
        v_new = u_val - both[:, :c]
        kd_t = jnp.swapaxes(k * jnp.exp(glast - gcol), 1, 2)
        tail = _bmm(jnp.concatenate([qk, kd_t], axis=1), v_new)
        o = both[:, c:] + tail[:, :c]
        o = o * lax.rsqrt(jnp.mean(o * o, axis=-1, keepdims=True) + EPS) * ng_ref[...]
        return o, state * jnp.exp(glast) + tail[:, c:]

    state = state_ref[...]
    for r0 in range(0, q_ref.shape[0], c):
        o, state = chunk(r0, state)
        for h in range(DN_HEADS):
            sl = slice(h * DN_HD, (h + 1) * DN_HD)
            o_ref[r0:r0 + c, sl] = (o[h] * _silu(z_ref[r0:r0 + c, sl])).astype(o_ref.dtype)
    state_ref[...] = state


def _deltanet(p_main, gates, gates_t, conv_w, norm_g, bsz, seq):
    c = DN_CHUNK * DN_CHUNKS_PER_STEP
    ns = seq // c
    row = lambda b, i: b * ns + i
    halo_row = lambda b, i: jnp.maximum((b * ns + i) * (c // SUBLANES) - 1, 0)
    cur = lambda cb: pl.BlockSpec((c, DN_WIDTH), lambda b, i: (row(b, i), cb))
    halo = lambda cb: pl.BlockSpec((SUBLANES, DN_WIDTH), lambda b, i: (halo_row(b, i), cb))
    wspec = lambda cb: pl.BlockSpec((DN_CONV, DN_WIDTH), lambda b, i: (0, cb))
    return pl.pallas_call(
        _dn_kernel,
        grid=(bsz, ns),
        in_specs=[cur(0), cur(1), cur(2), halo(0), halo(1), halo(2), cur(3),
                  pl.BlockSpec((c, LANES), lambda b, i: (row(b, i), 0)),
                  pl.BlockSpec((LANES, c), lambda b, i: (0, row(b, i))),
                  wspec(0), wspec(1), wspec(2),
                  pl.BlockSpec((1, DN_HD), lambda b, i: (0, 0))],
        out_specs=pl.BlockSpec((c, DN_WIDTH), lambda b, i: (row(b, i), 0)),
        out_shape=jax.ShapeDtypeStruct((bsz * seq, DN_WIDTH), BF16),
        scratch_shapes=[pltpu.VMEM((DN_HEADS, DN_HD, DN_HD), F32)],
        compiler_params=_params("parallel", "arbitrary"),
        name="deltanet",
    )(p_main, p_main, p_main, p_main, p_main, p_main, p_main, gates, gates_t,
      conv_w, conv_w, conv_w, norm_g.reshape(1, DN_HD))


def _cv_kernel(a_ref, b_ref, ha_ref, hb_ref, z_ref, w_ref, bias_ref, lg_ref, lb_ref, o_ref):
    i = pl.program_id(1)
    ts = a_ref.shape[0]
    glu_cur = a_ref[...] * _sigmoid(b_ref[...])
    glu_halo = jnp.where(i == 0, 0.0, ha_ref[...] * _sigmoid(hb_ref[...]))
    x = jnp.concatenate([glu_halo, glu_cur], axis=0)
    w = w_ref[...]
    shifted = [x] + [pltpu.roll(x, r, axis=0) for r in range(1, SUBLANES)]
    acc = None
    for tap in range(CV_KERNEL):
        back = CV_KERNEL - 1 - tap
        start = CV_HALO - (back // SUBLANES) * SUBLANES
        term = shifted[back % SUBLANES][start:start + ts, :] * w[tap:tap + 1, :]
        acc = term if acc is None else acc + term
    y = acc + bias_ref[...]
    mu = jnp.mean(y, axis=-1, keepdims=True)
    yc = y - mu
    y = yc * lax.rsqrt(jnp.mean(yc * yc, axis=-1, keepdims=True) + EPS) * lg_ref[...] + lb_ref[...]
    o_ref[...] = (_silu(y) * _silu(z_ref[...])).astype(o_ref.dtype)


def _conformer(p_main, cb_a, cb_b, cb_z, dw_w, dw_b, ln_g, ln_b, bsz, seq, ts=256):
    ts = min(ts, seq)
    ns = seq // ts
    row = lambda b, i: b * ns + i
    halo_row = lambda b, i: jnp.maximum((b * ns + i) * (ts // CV_HALO) - 1, 0)
    cur = lambda cb: pl.BlockSpec((ts, CV_WIDTH), lambda b, i: (row(b, i), cb))
    halo = lambda cb: pl.BlockSpec((CV_HALO, CV_WIDTH), lambda b, i: (halo_row(b, i), cb))
    vec = pl.BlockSpec((1, CV_WIDTH), lambda b, i: (0, 0))
    return pl.pallas_call(
        _cv_kernel,
        grid=(bsz, ns),
        in_specs=[cur(cb_a), cur(cb_b), halo(cb_a), halo(cb_b), cur(cb_z),
                  pl.BlockSpec((CV_KERNEL, CV_WIDTH), lambda b, i: (0, 0)), vec, vec, vec],
        out_specs=pl.BlockSpec((ts, CV_WIDTH), lambda b, i: (row(b, i), 0)),
        out_shape=jax.ShapeDtypeStruct((bsz * seq, CV_WIDTH), BF16),
        compiler_params=_params("parallel", "parallel"),
        name="conformer_conv",
    )(p_main, p_main, p_main, p_main, p_main, dw_w,
      dw_b.reshape(1, CV_WIDTH), ln_g.reshape(1, CV_WIDTH), ln_b.reshape(1, CV_WIDTH))


def _mem_kernel(q_ref, z_ref, k_ref, v_ref, qg_ref, kg_ref, o_ref):
    for h in range(MEM_HEADS):
        sl = slice(h * MEM_HD, (h + 1) * MEM_HD)
        q = q_ref[:, sl]
        qn = q * lax.rsqrt(jnp.mean(q * q, axis=-1, keepdims=True) + EPS) * qg_ref[...]
        k = k_ref[:, sl]
        kn = k * lax.rsqrt(jnp.mean(k * k, axis=-1, keepdims=True) + EPS) * kg_ref[...]
        s = _mm_nt(qn * (MEM_HD ** -0.5 * LOG2E), kn)
        p = jnp.exp2(s - jnp.max(s, axis=-1, keepdims=True))
        o = _mm(p, v_ref[:, sl]) * (1.0 / jnp.sum(p, axis=-1, keepdims=True))
        o_ref[:, sl] = (o * _silu(z_ref[:, sl])).astype(o_ref.dtype)


def _mem_attend(p, cb_q, cb_z, mkv, q_norm_g, k_norm_g, bsz, seq, ts=512):
    ts = min(ts, seq)
    ns = seq // ts
    mlen = mkv.shape[0] // bsz
    vec = pl.BlockSpec((1, MEM_HD), lambda b, i: (0, 0))
    return pl.pallas_call(
        _mem_kernel,
        grid=(bsz, ns),
        in_specs=[pl.BlockSpec((ts, MEM_WIDTH), lambda b, i: (b * ns + i, cb_q)),
                  pl.BlockSpec((ts, MEM_WIDTH), lambda b, i: (b * ns + i, cb_z)),
                  pl.BlockSpec((mlen, MEM_WIDTH), lambda b, i: (b, 0)),
                  pl.BlockSpec((mlen, MEM_WIDTH), lambda b, i: (b, 1)),
                  vec, vec],
        out_specs=pl.BlockSpec((ts, MEM_WIDTH), lambda b, i: (b * ns + i, 0)),
        out_shape=jax.ShapeDtypeStruct((bsz * seq, MEM_WIDTH), BF16),
        compiler_params=_params("parallel", "parallel"),
        name="mem_attend",
    )(p, p, mkv, mkv, q_norm_g.reshape(1, MEM_HD), k_norm_g.reshape(1, MEM_HD))


def _sb_kernel(q_ref, k_ref, v_ref, z_ref, qg_ref, kg_ref, o_ref, kn_ref, vb_ref, *, nq):
    i = pl.program_id(2)
    tk = SB_TK
    tq = q_ref.shape[0]
    head_lanes = [slice(hh * SB_HD, (hh + 1) * SB_HD) for hh in range(SB_HEADS_PER_STEP)]

    @pl.when(i == 0)
    def _():
        for hl in head_lanes:
            k = k_ref[:, hl]
            kn = k * lax.rsqrt(jnp.mean(k * k, axis=-1, keepdims=True) + EPS) * kg_ref[...]
            kn_ref[:, hl] = kn.astype(BF16)
        vb_ref[...] = v_ref[...].astype(BF16)

    qns = []
    for hl in head_lanes:
        q = q_ref[:, hl]
        qn = q * lax.rsqrt(jnp.mean(q * q, axis=-1, keepdims=True) + EPS) * qg_ref[...]
        qns.append((qn * (SB_HD ** -0.5 * LOG2E)).astype(BF16))
    nsub = tq // tk

    uj = lax.broadcasted_iota(jnp.int32, (2 * tk, tk + LANES), 0) & (tk - 1)
    us = lax.broadcasted_iota(jnp.int32, (2 * tk, tk + LANES), 1)
    u_aug = jnp.where((uj >= us) | (us >= tk), 1.0, 0.0).astype(BF16)

    sign_bit = jnp.uint32(0x80000000)

    def sub_block(z, vis):
        neg_abs = lax.bitcast_convert_type(lax.bitcast_convert_type(z, jnp.uint32) | sign_bit, F32)
        cost = jnp.maximum(z, 0.0) + jnp.log2(1.0 + jnp.exp2(neg_abs))
        if vis is not None:
            cost = jnp.where(vis, cost, 0.0)
        hi = cost.astype(BF16)
        lo = (cost - hi.astype(F32)).astype(BF16)
        inc = jnp.dot(jnp.concatenate([hi, lo], axis=1), u_aug, preferred_element_type=F32)
        return z - inc[:, :tk], inc[:, tk:]

    def key_block(hh, key_start, run, diagonal):
        kb = kn_ref[pl.ds(key_start, tq), head_lanes[hh]]
        vb = vb_ref[pl.ds(key_start, tq), head_lanes[hh]]
        z_all = lax.dot_general(qns[hh], kb, (((1,), (1,)), ((), ())),
                                preferred_element_type=F32)
        ws = [None] * nsub
        for s in range(nsub - 1, -1, -1):
            vis = None
            if diagonal:
                vis = (lax.broadcasted_iota(jnp.int32, (tq, tk), 1) + s * tk
                       < lax.broadcasted_iota(jnp.int32, (tq, tk), 0))
            pre, tot = sub_block(z_all[:, s * tk:(s + 1) * tk], vis)
            w = jnp.exp2(pre + run)
            if diagonal:
                w = jnp.where(vis, w, 0.0)
            ws[s] = w.astype(BF16)
            run = run - tot
        return jnp.dot(jnp.concatenate(ws, axis=1), vb, preferred_element_type=F32), run

    def query_block(qi):
        runs = [jnp.zeros((tq, LANES), F32) for _ in head_lanes]
        accs = [None for _ in head_lanes]
        for kj in range(qi, -1, -1):
            for hh in range(len(head_lanes)):
                pv, runs[hh] = key_block(hh, kj * tq, runs[hh], kj == qi)
                accs[hh] = pv if accs[hh] is None else accs[hh] + pv
        for hh, hl in enumerate(head_lanes):
            o_ref[:, hl] = (accs[hh] * _silu(z_ref[:, hl])).astype(o_ref.dtype)

    for qi in range(nq):
        pl.when(i == qi)(functools.partial(query_block, qi))


def _stickbreak(p, q_norm_g, k_norm_g, bsz, seq):
    tq = min(SB_TQ, seq)
    nq = seq // tq
    ng = SB_HEADS // SB_HEADS_PER_STEP
    gw = SB_HEADS_PER_STEP * SB_HD
    vec = pl.BlockSpec((1, SB_HD), lambda b, h, i: (0, 0))
    return pl.pallas_call(
        functools.partial(_sb_kernel, nq=nq),
        grid=(bsz, ng, nq),
        in_specs=[pl.BlockSpec((tq, gw), lambda b, h, i: (b * nq + i, h)),
                  pl.BlockSpec((seq, gw), lambda b, h, i: (b, ng + h)),
                  pl.BlockSpec((seq, gw), lambda b, h, i: (b, 2 * ng + h)),
                  pl.BlockSpec((tq, gw), lambda b, h, i: (b * nq + i, 3 * ng + h)),
                  vec, vec],
        out_specs=pl.BlockSpec((tq, gw), lambda b, h, i: (b * nq + i, h)),
        out_shape=jax.ShapeDtypeStruct((bsz * seq, SB_WIDTH), BF16),
        scratch_shapes=[pltpu.VMEM((seq, gw), BF16), pltpu.VMEM((seq, gw), BF16)],
        compiler_params=_params("parallel", "parallel", "arbitrary"),
        name="stickbreak",
    )(p, p, p, p, q_norm_g.reshape(1, SB_HD), k_norm_g.reshape(1, SB_HD))


def _even_layer(h, xn, mkv, k_norm_g, j, w_in, conv_qkv, a_log, dt_bias, dn_norm_g, dw_w, dw_b,
                ln_g, ln_b, q_norm_m, w_out, next_norm_g, bsz, seq):
    n_main = w_in.shape[2] - 2 * DN_HEADS
    w_in_t = jnp.swapaxes(w_in, 1, 2)
    p_main = _proj(xn, w_in_t, j, n_main, transposed=True,
                   skip_from=4 * DN_WIDTH // 1024, skip=2 * DN_HEADS, tn=1024)
    ba = _proj(xn, w_in_t, j, LANES, col0=4 * DN_WIDTH // LANES, transposed=True, tn=LANES)
    gates, gates_t = _gates(ba, a_log, dt_bias)
    o_a = _deltanet(p_main, gates, gates_t, conv_qkv, dn_norm_g, bsz, seq)
    o_b = _conformer(p_main, 4, 5, 6, dw_w, dw_b, ln_g, ln_b, bsz, seq)
    o_m = _mem_attend(p_main, 7, 8, mkv, q_norm_m, k_norm_g, bsz, seq)
    return _out_proj([o_a, o_b, o_m], w_out, j, [0, 1, 2], h, next_norm_g)


def _odd_layer(h, xn, mkv, k_norm_g, j, w_in, q_norm_c, k_norm_c, q_norm_m, w_out, next_norm_g,
               bsz, seq):
    p = _proj(xn, w_in, j, w_in.shape[2])
    o_c = _stickbreak(p, q_norm_c, k_norm_c, bsz, seq)
    o_m = _mem_attend(p, 8, 9, mkv, q_norm_m, k_norm_g, bsz, seq)
    return _out_proj([o_c, o_m], w_out, j, [0, SB_WIDTH // MEM_WIDTH], h, next_norm_g)


def kernel(x, mem, mem_norm_g, w_mem_kv, mem_k_norm_g, ev_norm_g, ev_w_in, ev_conv_qkv, ev_a_log,
           ev_dt_bias, ev_dn_norm_g, ev_dw_w, ev_dw_b, ev_ln_g, ev_ln_b, ev_q_norm_m, ev_w_out,
           od_norm_g, od_w_in, od_q_norm_c, od_k_norm_c, od_q_norm_m, od_w_out):
    bsz, seq, d = x.shape
    mlen = mem.shape[1]
    depth = ev_norm_g.shape[0] + od_norm_g.shape[0]
    mkv = _proj(_rmsnorm_bf16(mem.reshape(bsz * mlen, d), mem_norm_g, tm=256),
                w_mem_kv[None], 0, w_mem_kv.shape[1])
    h = x.reshape(bsz * seq, d)
    layer_gain = lambda layer: (ev_norm_g if layer % 2 == 0 else od_norm_g)[layer // 2]
    xn = _rmsnorm_bf16(h, layer_gain(0))
    for layer in range(depth):
        j = layer // 2
        next_g = layer_gain(layer + 1) if layer + 1 < depth else None
        if layer % 2 == 0:
            h, xn = _even_layer(h, xn, mkv, mem_k_norm_g, j, ev_w_in, ev_conv_qkv[j], ev_a_log[j],
                                ev_dt_bias[j], ev_dn_norm_g[j], ev_dw_w[j], ev_dw_b[j], ev_ln_g[j],
                                ev_ln_b[j], ev_q_norm_m[j], ev_w_out, next_g, bsz, seq)
        else:
            h, xn = _odd_layer(h, xn, mkv, mem_k_norm_g, j, od_w_in, od_q_norm_c[j], od_k_norm_c[j],
                               od_q_norm_m[j], od_w_out, next_g, bsz, seq)
    return h.reshape(bsz, seq, d)
```

```python
import functools

import jax
import jax.numpy as jnp
from jax import lax
from jax.experimental import pallas as pl
from jax.experimental.pallas import tpu as pltpu

F32 = jnp.float32
BF16 = jnp.bfloat16
EPS = 1e-6
LOG2E = 1.4426950408889634

LANES = 128
SUBLANES = 8
VMEM_LIMIT = 48 * 1024 * 1024

DN_HEADS = 8
DN_HD = 128
DN_WIDTH = DN_HEADS * DN_HD
DN_CONV = 4
DN_CHUNK = 128
DN_CHUNKS_PER_STEP = 2
DN_SPLIT_MAX_POWER = 8
CV_WIDTH = 1024
CV_KERNEL = 31
CV_HALO = 32
SB_HEADS = 16
SB_HD = 128
SB_WIDTH = SB_HEADS * SB_HD
SB_HEADS_PER_STEP = 2
SB_TQ = 512
SB_TK = 128
SB_UNDERFLOW_LOG2 = -160.0
MEM_HEADS = 4
MEM_HD = 256
MEM_WIDTH = MEM_HEADS * MEM_HD


def _params(*sem):
    return pltpu.CompilerParams(dimension_semantics=sem, vmem_limit_bytes=VMEM_LIMIT)


def _sigmoid(x):
    return 1.0 / (1.0 + jnp.exp(-x))


def _silu(x):
    return x * _sigmoid(x)


def _softplus(x):
    return jnp.maximum(x, 0.0) + jnp.log(1.0 + jnp.exp(-jnp.abs(x)))


def _mm(a, b):
    return jnp.dot(a.astype(BF16), b.astype(BF16), preferred_element_type=F32)


def _mm_nt(a, b):
    return lax.dot_general(a.astype(BF16), b.astype(BF16), (((1,), (1,)), ((), ())),
                           preferred_element_type=F32)


def _bmm(a, b):
    return lax.dot_general(a.astype(BF16), b.astype(BF16), (((2,), (1,)), ((0,), (0,))),
                           preferred_element_type=F32)


def _bmm3(a, b):
    a_hi = a.astype(BF16)
    b_hi = b.astype(BF16)
    a_lo = (a - a_hi.astype(F32)).astype(BF16)
    b_lo = (b - b_hi.astype(F32)).astype(BF16)
    lhs = jnp.concatenate([a_hi, a_hi, a_lo], axis=2)
    rhs = jnp.concatenate([b_hi, b_lo, b_hi], axis=1)
    return lax.dot_general(lhs, rhs, (((2,), (1,)), ((0,), (0,))), preferred_element_type=F32)


def _bmm_nt(a, b):
    return lax.dot_general(a.astype(BF16), b.astype(BF16), (((2,), (2,)), ((0,), (0,))),
                           preferred_element_type=F32)


def _rmsnorm_kernel(x_ref, g_ref, o_ref):
    x = x_ref[...]
    ms = jnp.mean(x * x, axis=-1, keepdims=True)
    o_ref[...] = (x * lax.rsqrt(ms + EPS) * g_ref[...]).astype(o_ref.dtype)


def _rmsnorm_bf16(x2d, g, tm=512):
    m, d = x2d.shape
    return pl.pallas_call(
        _rmsnorm_kernel,
        grid=(m // tm,),
        in_specs=[pl.BlockSpec((tm, d), lambda i: (i, 0)),
                  pl.BlockSpec((1, d), lambda i: (0, 0))],
        out_specs=pl.BlockSpec((tm, d), lambda i: (i, 0)),
        out_shape=jax.ShapeDtypeStruct((m, d), BF16),
        compiler_params=_params("parallel"),
        name="rmsnorm",
    )(x2d, g.reshape(1, d))


PROJ_CAST_ROWS = 256


def _cast_rows(src, dst, offset, tail):
    nrows = src.shape[0]
    rows = min(PROJ_CAST_ROWS, nrows)
    nchunks = nrows // rows

    def body(r, carry):
        start = r * rows
        dst[pl.ds(pl.multiple_of(start, rows), rows), :] = (
            src[pl.ds(pl.multiple_of(start + offset, SUBLANES), rows), :].astype(BF16))
        return carry

    lax.fori_loop(0, nchunks - 1 if offset else nchunks, body, 0)
    if offset:
        last = (nchunks - 1) * rows
        dst[last:nrows - offset, :] = src[last + offset:, :].astype(BF16)
        dst[nrows - offset:, :] = tail[...].astype(BF16)


def _proj_kernel(*refs, transposed, skip_from, skip):
    j = pl.program_id(0)
    i = pl.program_id(1)
    x_ref, w_ref = refs[:2]
    tail_ref = refs[2] if skip_from is not None else None
    o_ref, wb_ref = refs[-2:]

    if skip_from is None:
        pl.when(i == 0)(lambda: _cast_rows(w_ref, wb_ref, 0, None))
    else:
        pl.when((i == 0) & (j < skip_from))(lambda: _cast_rows(w_ref, wb_ref, 0, None))
        pl.when((i == 0) & (j >= skip_from))(lambda: _cast_rows(w_ref, wb_ref, skip, tail_ref))

    contract = (((1,), (1,)), ((), ())) if transposed else (((1,), (0,)), ((), ()))
    o_ref[...] = lax.dot_general(x_ref[...], wb_ref[...], contract, preferred_element_type=F32)


def _proj(x, w, layer, n, col0=0, transposed=False, skip_from=None, skip=0, tm=1024, tn=1024):
    m, k = x.shape
    tm = min(tm, m)
    tn = min(tn, n)
    in_specs = [pl.BlockSpec((tm, k), lambda j, i: (i, 0))]
    if transposed:
        in_specs.append(pl.BlockSpec((None, tn, k), lambda j, i: (layer, col0 + j, 0)))
        scratch = [pltpu.VMEM((tn, k), BF16)]
    else:
        in_specs.append(pl.BlockSpec((None, k, tn), lambda j, i: (layer, 0, col0 + j)))
        scratch = [pltpu.VMEM((k, tn), BF16)]
    args = [x, w]
    if skip_from is not None:
        in_specs.append(pl.BlockSpec((None, skip, k),
                                     lambda j, i: (layer, (col0 + j + 1) * (tn // skip), 0)))
        args.append(w)
    return pl.pallas_call(
        functools.partial(_proj_kernel, transposed=transposed, skip_from=skip_from, skip=skip),
        grid=(n // tn, m // tm),
        in_specs=in_specs,
        out_specs=pl.BlockSpec((tm, tn), lambda j, i: (i, j)),
        out_shape=jax.ShapeDtypeStruct((m, n), F32),
        scratch_shapes=scratch,
        compiler_params=_params("parallel", "arbitrary"),
        name="proj",
    )(*args)


def _out_proj_kernel(*refs, n_pairs, emit_norm, nj):
    i = pl.program_id(0)
    j = pl.program_id(1)
    xs = refs[:n_pairs]
    ws = refs[n_pairs:2 * n_pairs]
    res_ref = refs[2 * n_pairs]
    pos = 2 * n_pairs + 1
    g_ref = refs[pos] if emit_norm else None
    pos += int(emit_norm)
    o_ref = refs[pos]
    xn_ref = refs[pos + 1] if emit_norm else None
    pos += 1 + int(emit_norm)
    wb = refs[pos:pos + n_pairs]
    rows_ref = refs[pos + n_pairs] if emit_norm else None

    @pl.when(i == 0)
    def _():
        for p in range(n_pairs):
            _cast_rows(ws[p], wb[p].at[j], 0, None)

    acc = None
    for p in range(n_pairs):
        d = jnp.dot(xs[p][...], wb[p][j], preferred_element_type=F32)
        acc = d if acc is None else acc + d
    acc = res_ref[...] + acc
    o_ref[...] = acc
    if emit_norm:
        rows_ref[j] = acc

        @pl.when(j == nj - 1)
        def _():
            tn = acc.shape[1]
            ssq = None
            for jj in range(nj):
                t = rows_ref[jj]
                s = jnp.sum(t * t, axis=-1, keepdims=True)
                ssq = s if ssq is None else ssq + s
            scale = lax.rsqrt(ssq * (1.0 / (nj * tn)) + EPS)
            for jj in range(nj):
                sl = slice(jj * tn, (jj + 1) * tn)
                xn_ref[:, sl] = (rows_ref[jj] * scale * g_ref[:, sl]).astype(xn_ref.dtype)


def _out_proj(xs, w, layer, row_blocks, res, next_norm_g=None, tm=512, tn=512):
    m, n = res.shape
    nj = n // tn
    emit_norm = next_norm_g is not None
    in_specs = [pl.BlockSpec((tm, x.shape[1]), lambda i, j: (i, 0)) for x in xs]
    in_specs += [pl.BlockSpec((None, x.shape[1], tn),
                              lambda i, j, rb=rb: (layer, rb, jnp.where(i == 0, j, nj - 1)))
                 for x, rb in zip(xs, row_blocks)]
    in_specs.append(pl.BlockSpec((tm, tn), lambda i, j: (i, j)))
    args = list(xs) + [w] * len(xs) + [res]
    out_specs = [pl.BlockSpec((tm, tn), lambda i, j: (i, j))]
    out_shape = [jax.ShapeDtypeStruct((m, n), F32)]
    scratch = [pltpu.VMEM((nj, x.shape[1], tn), BF16) for x in xs]
    if emit_norm:
        in_specs.append(pl.BlockSpec((1, n), lambda i, j: (0, 0)))
        args.append(next_norm_g.reshape(1, n))
        out_specs.append(pl.BlockSpec((tm, n), lambda i, j: (i, 0)))
        out_shape.append(jax.ShapeDtypeStruct((m, n), BF16))
        scratch.append(pltpu.VMEM((nj, tm, tn), F32))
    outs = pl.pallas_call(
        functools.partial(_out_proj_kernel, n_pairs=len(xs), emit_norm=emit_norm, nj=nj),
        grid=(m // tm, nj),
        in_specs=in_specs,
        out_specs=out_specs,
        out_shape=out_shape,
        scratch_shapes=scratch,
        compiler_params=_params("arbitrary", "arbitrary"),
        name="out_proj",
    )(*args)
    return (outs[0], outs[1]) if emit_norm else (outs[0], None)


def _gates_kernel(ba_ref, alog_ref, dt_ref, g_ref, gt_ref):
    x = ba_ref[...]
    col = lax.broadcasted_iota(jnp.int32, x.shape, 1)
    row = lax.broadcasted_iota(jnp.int32, x.shape, 0)
    beta = _sigmoid(x)
    g = -jnp.exp(alog_ref[...]) * _softplus(x + dt_ref[...])
    rin = row & (DN_CHUNK - 1)
    gc = g
    s = 1
    while s < DN_CHUNK:
        gc = gc + jnp.where(rin >= s, pltpu.roll(gc, s, axis=0), 0.0)
        s *= 2
    out = jnp.where(col < DN_HEADS, beta,
                    jnp.where(col < 2 * DN_HEADS, g, pltpu.roll(gc, DN_HEADS, axis=1)))
    g_ref[...] = out
    gt_ref[...] = out.T


def _gates(ba, a_log, dt_bias, ts=512):
    m = ba.shape[0]
    ts = min(ts, m)
    pad = lambda v: jnp.zeros((1, LANES), F32).at[0, DN_HEADS:2 * DN_HEADS].set(v.astype(F32))
    return pl.pallas_call(
        _gates_kernel,
        grid=(m // ts,),
        in_specs=[pl.BlockSpec((ts, LANES), lambda i: (i, 0)),
                  pl.BlockSpec((1, LANES), lambda i: (0, 0)),
                  pl.BlockSpec((1, LANES), lambda i: (0, 0))],
        out_specs=[pl.BlockSpec((ts, LANES), lambda i: (i, 0)),
                   pl.BlockSpec((LANES, ts), lambda i: (0, i))],
        out_shape=[jax.ShapeDtypeStruct((m, LANES), F32),
                   jax.ShapeDtypeStruct((LANES, m), F32)],
        compiler_params=_params("parallel"),
        name="dn_gates",
    )(ba, pad(a_log), pad(dt_bias))


def _dn_kernel(q_ref, k_ref, v_ref, hq_ref, hk_ref, hv_ref, z_ref, g_ref, gt_ref,
               wq_ref, wk_ref, wv_ref, ng_ref, o_ref, state_ref):
    i = pl.program_id(1)
    c = DN_CHUNK

    @pl.when(i == 0)
    def _():
        state_ref[...] = jnp.zeros_like(state_ref)

    def conv_act(cur_ref, halo_ref, w_ref):
        halo = jnp.where(i == 0, 0.0, halo_ref[...])
        x = jnp.concatenate([halo, cur_ref[...]], axis=0)
        w = w_ref[...]
        y = x * w[DN_CONV - 1:DN_CONV, :]
        for tap in range(DN_CONV - 1):
            y = y + pltpu.roll(x, DN_CONV - 1 - tap, axis=0) * w[tap:tap + 1, :]
        return _silu(y[SUBLANES:, :])

    q_all = conv_act(q_ref, hq_ref, wq_ref)
    k_all = conv_act(k_ref, hk_ref, wk_ref)
    v_all = conv_act(v_ref, hv_ref, wv_ref)
    ri = lax.broadcasted_iota(jnp.int32, (DN_HEADS, c, c), 1)
    ci = lax.broadcasted_iota(jnp.int32, (DN_HEADS, c, c), 2)
    heads = lambda t: jnp.stack([t[:, h * DN_HD:(h + 1) * DN_HD] for h in range(DN_HEADS)])

    def chunk(r0, state):
        q = heads(q_all[r0:r0 + c])
        k = heads(k_all[r0:r0 + c])
        v = heads(v_all[r0:r0 + c])
        q = q * lax.rsqrt(jnp.sum(q * q, axis=-1, keepdims=True) + EPS) * (DN_HD ** -0.5)
        k = k * lax.rsqrt(jnp.sum(k * k, axis=-1, keepdims=True) + EPS)
        gates = g_ref[r0:r0 + c, :]
        gates_t = gt_ref[:, r0:r0 + c]
        beta = jnp.stack([gates[:, h:h + 1] for h in range(DN_HEADS)])
        gcol = jnp.stack([gates[:, 2 * DN_HEADS + h:2 * DN_HEADS + h + 1]
                          for h in range(DN_HEADS)])
        grow = jnp.stack([gates_t[2 * DN_HEADS + h:2 * DN_HEADS + h + 1, :]
                          for h in range(DN_HEADS)])
        glast = grow[:, :, c - 1:c]
        decay = jnp.exp(jnp.minimum(gcol - grow, 0.0))
        kb = k * beta
        s2 = _bmm_nt(jnp.concatenate([kb, q], axis=1), k)
        a_mat = jnp.where(ri > ci, s2[:, :c] * decay, 0.0)
        qk = jnp.where(ri >= ci, s2[:, c:] * decay, 0.0)
        x = _bmm3(a_mat, a_mat)
        n = -a_mat
        p = 2
        while 2 * p < c:
            mm = _bmm3 if p <= DN_SPLIT_MAX_POWER else _bmm
            both = mm(x, jnp.concatenate([x, n], axis=-1))
            n = n + x + both[:, :, c:]
            x = both[:, :, :c]
            p *= 2
        n = n + x + _bmm(x, n)
        egc = jnp.exp(gcol)
        rhs = jnp.concatenate([v * beta, kb * egc], axis=-1)
        sol = rhs + _bmm(n, rhs)
        u_val, w_key = sol[:, :, :DN_HD], sol[:, :, DN_HD:]
        both = _bmm(jnp.concatenate([w_key, q * egc], axis=1), state)
        v_new = u_val - both[:, :c]
        kd_t = jnp.swapaxes(k * jnp.exp(glast - gcol), 1, 2)
        tail = _bmm(jnp.concatenate([qk, kd_t], axis=1), v_new)
        o = both[:, c:] + tail[:, :c]
        o = o * lax.rsqrt(jnp.mean(o * o, axis=-1, keepdims=True) + EPS) * ng_ref[...]
        return o, state * jnp.exp(glast) + tail[:, c:]

    state = state_ref[...]
    for r0 in range(0, q_ref.shape[0], c):
        o, state = chunk(r0, state)
        for h in range(DN_HEADS):
            sl = slice(h * DN_HD, (h + 1) * DN_HD)
            o_ref[r0:r0 + c, sl] = (o[h] * _silu(z_ref[r0:r0 + c, sl])).astype(o_ref.dtype)
    state_ref[...] = state


def _deltanet(p_main, gates, gates_t, conv_w, norm_g, bsz, seq):
    c = DN_CHUNK * DN_CHUNKS_PER_STEP
    ns = seq // c
    row = lambda b, i: b * ns + i
    halo_row = lambda b, i: jnp.maximum((b * ns + i) * (c // SUBLANES) - 1, 0)
    cur = lambda cb: pl.BlockSpec((c, DN_WIDTH), lambda b, i: (row(b, i), cb))
    halo = lambda cb: pl.BlockSpec((SUBLANES, DN_WIDTH), lambda b, i: (halo_row(b, i), cb))
    wspec = lambda cb: pl.BlockSpec((DN_CONV, DN_WIDTH), lambda b, i: (0, cb))
    return pl.pallas_call(
        _dn_kernel,
        grid=(bsz, ns),
        in_specs=[cur(0), cur(1), cur(2), halo(0), halo(1), halo(2), cur(3),
                  pl.BlockSpec((c, LANES), lambda b, i: (row(b, i), 0)),
                  pl.BlockSpec((LANES, c), lambda b, i: (0, row(b, i))),
                  wspec(0), wspec(1), wspec(2),
                  pl.BlockSpec((1, DN_HD), lambda b, i: (0, 0))],
        out_specs=pl.BlockSpec((c, DN_WIDTH), lambda b, i: (row(b, i), 0)),
        out_shape=jax.ShapeDtypeStruct((bsz * seq, DN_WIDTH), BF16),
        scratch_shapes=[pltpu.VMEM((DN_HEADS, DN_HD, DN_HD), F32)],
        compiler_params=_params("parallel", "arbitrary"),
        name="deltanet",
    )(p_main, p_main, p_main, p_main, p_main, p_main, p_main, gates, gates_t,
      conv_w, conv_w, conv_w, norm_g.reshape(1, DN_HD))


def _cv_kernel(a_ref, b_ref, ha_ref, hb_ref, z_ref, w_ref, bias_ref, lg_ref, lb_ref, o_ref):
    i = pl.program_id(1)
    ts = a_ref.shape[0]
    glu_cur = a_ref[...] * _sigmoid(b_ref[...])
    glu_halo = jnp.where(i == 0, 0.0, ha_ref[...] * _sigmoid(hb_ref[...]))
    x = jnp.concatenate([glu_halo, glu_cur], axis=0)
    w = w_ref[...]
    shifted = [x] + [pltpu.roll(x, r, axis=0) for r in range(1, SUBLANES)]
    acc = None
    for tap in range(CV_KERNEL):
        back = CV_KERNEL - 1 - tap
        start = CV_HALO - (back // SUBLANES) * SUBLANES
        term = shifted[back % SUBLANES][start:start + ts, :] * w[tap:tap + 1, :]
        acc = term if acc is None else acc + term
    y = acc + bias_ref[...]
    mu = jnp.mean(y, axis=-1, keepdims=True)
    yc = y - mu
    y = yc * lax.rsqrt(jnp.mean(yc * yc, axis=-1, keepdims=True) + EPS) * lg_ref[...] + lb_ref[...]
    o_ref[...] = (_silu(y) * _silu(z_ref[...])).astype(o_ref.dtype)


def _conformer(p_main, cb_a, cb_b, cb_z, dw_w, dw_b, ln_g, ln_b, bsz, seq, ts=256):
    ts = min(ts, seq)
    ns = seq // ts
    row = lambda b, i: b * ns + i
    halo_row = lambda b, i: jnp.maximum((b * ns + i) * (ts // CV_HALO) - 1, 0)
    cur = lambda cb: pl.BlockSpec((ts, CV_WIDTH), lambda b, i: (row(b, i), cb))
    halo = lambda cb: pl.BlockSpec((CV_HALO, CV_WIDTH), lambda b, i: (halo_row(b, i), cb))
    vec = pl.BlockSpec((1, CV_WIDTH), lambda b, i: (0, 0))
    return pl.pallas_call(
        _cv_kernel,
        grid=(bsz, ns),
        in_specs=[cur(cb_a), cur(cb_b), halo(cb_a), halo(cb_b), cur(cb_z),
                  pl.BlockSpec((CV_KERNEL, CV_WIDTH), lambda b, i: (0, 0)), vec, vec, vec],
        out_specs=pl.BlockSpec((ts, CV_WIDTH), lambda b, i: (row(b, i), 0)),
        out_shape=jax.ShapeDtypeStruct((bsz * seq, CV_WIDTH), BF16),
        compiler_params=_params("parallel", "parallel"),
        name="conformer_conv",
    )(p_main, p_main, p_main, p_main, p_main, dw_w,
      dw_b.reshape(1, CV_WIDTH), ln_g.reshape(1, CV_WIDTH), ln_b.reshape(1, CV_WIDTH))


def _mem_kernel(q_ref, z_ref, k_ref, v_ref, qg_ref, kg_ref, o_ref):
    for h in range(MEM_HEADS):
        sl = slice(h * MEM_HD, (h + 1) * MEM_HD)
        q = q_ref[:, sl]
        qn = q * lax.rsqrt(jnp.mean(q * q, axis=-1, keepdims=True) + EPS) * qg_ref[...]
        k = k_ref[:, sl]
        kn = k * lax.rsqrt(jnp.mean(k * k, axis=-1, keepdims=True) + EPS) * kg_ref[...]
        s = _mm_nt(qn * (MEM_HD ** -0.5 * LOG2E), kn)
        p = jnp.exp2(s - jnp.max(s, axis=-1, keepdims=True))
        o = _mm(p, v_ref[:, sl]) * (1.0 / jnp.sum(p, axis=-1, keepdims=True))
        o_ref[:, sl] = (o * _silu(z_ref[:, sl])).astype(o_ref.dtype)


def _mem_attend(p, cb_q, cb_z, mkv, q_norm_g, k_norm_g, bsz, seq, ts=512):
    ts = min(ts, seq)
    ns = seq // ts
    mlen = mkv.shape[0] // bsz
    vec = pl.BlockSpec((1, MEM_HD), lambda b, i: (0, 0))
    return pl.pallas_call(
        _mem_kernel,
        grid=(bsz, ns),
        in_specs=[pl.BlockSpec((ts, MEM_WIDTH), lambda b, i: (b * ns + i, cb_q)),
                  pl.BlockSpec((ts, MEM_WIDTH), lambda b, i: (b * ns + i, cb_z)),
                  pl.BlockSpec((mlen, MEM_WIDTH), lambda b, i: (b, 0)),
                  pl.BlockSpec((mlen, MEM_WIDTH), lambda b, i: (b, 1)),
                  vec, vec],
        out_specs=pl.BlockSpec((ts, MEM_WIDTH), lambda b, i: (b * ns + i, 0)),
        out_shape=jax.ShapeDtypeStruct((bsz * seq, MEM_WIDTH), BF16),
        compiler_params=_params("parallel", "parallel"),
        name="mem_attend",
    )(p, p, mkv, mkv, q_norm_g.reshape(1, MEM_HD), k_norm_g.reshape(1, MEM_HD))


def _sb_kernel(q_ref, k_ref, v_ref, z_ref, qg_ref, kg_ref, o_ref, kn_ref, vb_ref, *, nq):
    i = pl.program_id(2)
    tk = SB_TK
    tq = q_ref.shape[0]
    head_lanes = [slice(hh * SB_HD, (hh + 1) * SB_HD) for hh in range(SB_HEADS_PER_STEP)]

    @pl.when(i == 0)
    def _():
        for hl in head_lanes:
            k = k_ref[:, hl]
            kn = k * lax.rsqrt(jnp.mean(k * k, axis=-1, keepdims=True) + EPS) * kg_ref[...]
            kn_ref[:, hl] = kn.astype(BF16)
        vb_ref[...] = v_ref[...].astype(BF16)

    qns = []
    for hl in head_lanes:
        q = q_ref[:, hl]
        qn = q * lax.rsqrt(jnp.mean(q * q, axis=-1, keepdims=True) + EPS) * qg_ref[...]
        qns.append((qn * (SB_HD ** -0.5 * LOG2E)).astype(BF16))
    nsub = tq // tk

    uj = lax.broadcasted_iota(jnp.int32, (2 * tk, tk + LANES), 0) & (tk - 1)
    us = lax.broadcasted_iota(jnp.int32, (2 * tk, tk + LANES), 1)
    u_aug = jnp.where((uj >= us) | (us >= tk), 1.0, 0.0).astype(BF16)

    sign_bit = jnp.uint32(0x80000000)

    def sub_block(z, vis):
        neg_abs = lax.bitcast_convert_type(lax.bitcast_convert_type(z, jnp.uint32) | sign_bit, F32)
        cost = jnp.maximum(z, 0.0) + jnp.log2(1.0 + jnp.exp2(neg_abs))
        if vis is not None:
            cost = jnp.where(vis, cost, 0.0)
        hi = cost.astype(BF16)
        lo = (cost - hi.astype(F32)).astype(BF16)
        inc = jnp.dot(jnp.concatenate([hi, lo], axis=1), u_aug, preferred_element_type=F32)
        return z - inc[:, :tk], inc[:, tk:]

    def key_block(hh, key_start, run, diagonal):
        kb = kn_ref[pl.ds(key_start, tq), head_lanes[hh]]
        vb = vb_ref[pl.ds(key_start, tq), head_lanes[hh]]
        z_all = lax.dot_general(qns[hh], kb, (((1,), (1,)), ((), ())),
                                preferred_element_type=F32)
        ws = [None] * nsub
        for s in range(nsub - 1, -1, -1):
            vis = None
            if diagonal:
                vis = (lax.broadcasted_iota(jnp.int32, (tq, tk), 1) + s * tk
                       < lax.broadcasted_iota(jnp.int32, (tq, tk), 0))
            pre, tot = sub_block(z_all[:, s * tk:(s + 1) * tk], vis)
            w = jnp.exp2(pre + run)
            if diagonal:
                w = jnp.where(vis, w, 0.0)
            ws[s] = w.astype(BF16)
            run = run - tot
        return jnp.dot(jnp.concatenate(ws, axis=1), vb, preferred_element_type=F32), run

    def query_block(qi):
        nh = len(head_lanes)
        runs = [jnp.zeros((tq, LANES), F32) for _ in range(nh)]
        accs = [jnp.zeros((tq, SB_HD), F32) for _ in range(nh)]

        def sweep(blocks, accs, runs):
            accs, runs = list(accs), list(runs)
            for kj in blocks:
                for hh in range(nh):
                    pv, runs[hh] = key_block(hh, kj * tq, runs[hh], kj == qi)
                    accs[hh] = accs[hh] + pv
            return accs, runs

        accs, runs = sweep(range(qi, max(qi - 2, -1), -1), accs, runs)
        if qi >= 2:
            live = functools.reduce(jnp.maximum, [jnp.max(r) for r in runs]) >= SB_UNDERFLOW_LOG2
            rest = lambda carry: tuple(sum(sweep(range(qi - 2, -1, -1), carry[:nh], carry[nh:]), []))
            carry = lax.cond(live, rest, lambda carry: carry, tuple(accs) + tuple(runs))
            accs = carry[:nh]
        for hh, hl in enumerate(head_lanes):
            o_ref[:, hl] = (accs[hh] * _silu(z_ref[:, hl])).astype(o_ref.dtype)

    for qi in range(nq):
        pl.when(i == qi)(functools.partial(query_block, qi))


def _stickbreak(p, q_norm_g, k_norm_g, bsz, seq):
    tq = min(SB_TQ, seq)
    nq = seq // tq
    ng = SB_HEADS // SB_HEADS_PER_STEP
    gw = SB_HEADS_PER_STEP * SB_HD
    vec = pl.BlockSpec((1, SB_HD), lambda b, h, i: (0, 0))
    return pl.pallas_call(
        functools.partial(_sb_kernel, nq=nq),
        grid=(bsz, ng, nq),
        in_specs=[pl.BlockSpec((tq, gw), lambda b, h, i: (b * nq + i, h)),
                  pl.BlockSpec((seq, gw), lambda b, h, i: (b, ng + h)),
                  pl.BlockSpec((seq, gw), lambda b, h, i: (b, 2 * ng + h)),
                  pl.BlockSpec((tq, gw), lambda b, h, i: (b * nq + i, 3 * ng + h)),
                  vec, vec],
        out_specs=pl.BlockSpec((tq, gw), lambda b, h, i: (b * nq + i, h)),
        out_shape=jax.ShapeDtypeStruct((bsz * seq, SB_WIDTH), BF16),
        scratch_shapes=[pltpu.VMEM((seq, gw), BF16), pltpu.VMEM((seq, gw), BF16)],
        compiler_params=_params("parallel", "parallel", "arbitrary"),
        name="stickbreak",
    )(p, p, p, p, q_norm_g.reshape(1, SB_HD), k_norm_g.reshape(1, SB_HD))


def _even_layer(h, xn, mkv, k_norm_g, j, w_in, conv_qkv, a_log, dt_bias, dn_norm_g, dw_w, dw_b,
                ln_g, ln_b, q_norm_m, w_out, next_norm_g, bsz, seq):
    n_main = w_in.shape[2] - 2 * DN_HEADS
    w_in_t = jnp.swapaxes(w_in, 1, 2)
    p_main = _proj(xn, w_in_t, j, n_main, transposed=True,
                   skip_from=4 * DN_WIDTH // 1024, skip=2 * DN_HEADS, tn=1024)
    ba = _proj(xn, w_in_t, j, LANES, col0=4 * DN_WIDTH // LANES, transposed=True, tn=LANES)
    gates, gates_t = _gates(ba, a_log, dt_bias)
    o_a = _deltanet(p_main, gates, gates_t, conv_qkv, dn_norm_g, bsz, seq)
    o_b = _conformer(p_main, 4, 5, 6, dw_w, dw_b, ln_g, ln_b, bsz, seq)
    o_m = _mem_attend(p_main, 7, 8, mkv, q_norm_m, k_norm_g, bsz, seq)
    return _out_proj([o_a, o_b, o_m], w_out, j, [0, 1, 2], h, next_norm_g)


def _odd_layer(h, xn, mkv, k_norm_g, j, w_in, q_norm_c, k_norm_c, q_norm_m, w_out, next_norm_g,
               bsz, seq):
    p = _proj(xn, w_in, j, w_in.shape[2])
    o_c = _stickbreak(p, q_norm_c, k_norm_c, bsz, seq)
    o_m = _mem_attend(p, 8, 9, mkv, q_norm_m, k_norm_g, bsz, seq)
    return _out_proj([o_c, o_m], w_out, j, [0, SB_WIDTH // MEM_WIDTH], h, next_norm_g)


def kernel(x, mem, mem_norm_g, w_mem_kv, mem_k_norm_g, ev_norm_g, ev_w_in, ev_conv_qkv, ev_a_log,
           ev_dt_bias, ev_dn_norm_g, ev_dw_w, ev_dw_b, ev_ln_g, ev_ln_b, ev_q_norm_m, ev_w_out,
           od_norm_g, od_w_in, od_q_norm_c, od_k_norm_c, od_q_norm_m, od_w_out):
    bsz, seq, d = x.shape
    mlen = mem.shape[1]
    depth = ev_norm_g.shape[0] + od_norm_g.shape[0]
    mkv = _proj(_rmsnorm_bf16(mem.reshape(bsz * mlen, d), mem_norm_g, tm=256),
                w_mem_kv[None], 0, w_mem_kv.shape[1])
    h = x.reshape(bsz * seq, d)
    layer_gain = lambda layer: (ev_norm_g if layer % 2 == 0 else od_norm_g)[layer // 2]
    xn = _rmsnorm_bf16(h, layer_gain(0))
    for layer in range(depth):
        j = layer // 2
        next_g = layer_gain(layer + 1) if layer + 1 < depth else None
        if layer % 2 == 0:
            h, xn = _even_layer(h, xn, mkv, mem_k_norm_g, j, ev_w_in, ev_conv_qkv[j], ev_a_log[j],
                                ev_dt_bias[j], ev_dn_norm_g[j], ev_dw_w[j], ev_dw_b[j], ev_ln_g[j],
                                ev_ln_b[j], ev_q_norm_m[j], ev_w_out, next_g, bsz, seq)
        else:
            h, xn = _odd_layer(h, xn, mkv, mem_k_norm_g, j, od_w_in, od_q_norm_c[j], od_k_norm_c[j],
                               od_q_norm_m[j], od_w_out, next_g, bsz, seq)
    return h.reshape(bsz, seq, d)
```

```python
import functools

import jax
import jax.numpy as jnp
from jax import lax
from jax.experimental import pallas as pl
from jax.experimental.pallas import tpu as pltpu

F32 = jnp.float32
BF16 = jnp.bfloat16
EPS = 1e-6
LOG2E = 1.4426950408889634

LANES = 128
SUBLANES = 8
VMEM_LIMIT = 48 * 1024 * 1024

DN_HEADS = 8
DN_HD = 128
DN_WIDTH = DN_HEADS * DN_HD
DN_CONV = 4
DN_CHUNK = 128
DN_CHUNKS_PER_STEP = 2
DN_SPLIT_MAX_POWER = 8
CV_WIDTH = 1024
CV_KERNEL = 31
CV_HALO = 32
SB_HEADS = 16
SB_HD = 128
SB_WIDTH = SB_HEADS * SB_HD
SB_HEADS_PER_STEP = 2
SB_TQ = 512
SB_TK = 128
SB_UNDERFLOW_LOG2 = -160.0
SB_ALWAYS_SUBS = 2
MEM_HEADS = 4
MEM_HD = 256
MEM_WIDTH = MEM_HEADS * MEM_HD


def _params(*sem):
    return pltpu.CompilerParams(dimension_semantics=sem, vmem_limit_bytes=VMEM_LIMIT)


def _sigmoid(x):
    return 1.0 / (1.0 + jnp.exp(-x))


def _silu(x):
    return x * _sigmoid(x)


def _softplus(x):
    return jnp.maximum(x, 0.0) + jnp.log(1.0 + jnp.exp(-jnp.abs(x)))


def _mm(a, b):
    return jnp.dot(a.astype(BF16), b.astype(BF16), preferred_element_type=F32)


def _mm_nt(a, b):
    return lax.dot_general(a.astype(BF16), b.astype(BF16), (((1,), (1,)), ((), ())),
                           preferred_element_type=F32)


def _bmm(a, b):
    return lax.dot_general(a.astype(BF16), b.astype(BF16), (((2,), (1,)), ((0,), (0,))),
                           preferred_element_type=F32)


def _bmm3(a, b):
    a_hi = a.astype(BF16)
    b_hi = b.astype(BF16)
    a_lo = (a - a_hi.astype(F32)).astype(BF16)
    b_lo = (b - b_hi.astype(F32)).astype(BF16)
    lhs = jnp.concatenate([a_hi, a_hi, a_lo], axis=2)
    rhs = jnp.concatenate([b_hi, b_lo, b_hi], axis=1)
    return lax.dot_general(lhs, rhs, (((2,), (1,)), ((0,), (0,))), preferred_element_type=F32)


def _bmm_nt(a, b):
    return lax.dot_general(a.astype(BF16), b.astype(BF16), (((2,), (2,)), ((0,), (0,))),
                           preferred_element_type=F32)


def _rmsnorm_kernel(x_ref, g_ref, o_ref):
    x = x_ref[...]
    ms = jnp.mean(x * x, axis=-1, keepdims=True)
    o_ref[...] = (x * lax.rsqrt(ms + EPS) * g_ref[...]).astype(o_ref.dtype)


def _rmsnorm_bf16(x2d, g, tm=512):
    m, d = x2d.shape
    return pl.pallas_call(
        _rmsnorm_kernel,
        grid=(m // tm,),
        in_specs=[pl.BlockSpec((tm, d), lambda i: (i, 0)),
                  pl.BlockSpec((1, d), lambda i: (0, 0))],
        out_specs=pl.BlockSpec((tm, d), lambda i: (i, 0)),
        out_shape=jax.ShapeDtypeStruct((m, d), BF16),
        compiler_params=_params("parallel"),
        name="rmsnorm",
    )(x2d, g.reshape(1, d))


PROJ_CAST_ROWS = 256


def _cast_rows(src, dst, offset, tail):
    nrows = src.shape[0]
    rows = min(PROJ_CAST_ROWS, nrows)
    nchunks = nrows // rows

    def body(r, carry):
        start = r * rows
        dst[pl.ds(pl.multiple_of(start, rows), rows), :] = (
            src[pl.ds(pl.multiple_of(start + offset, SUBLANES), rows), :].astype(BF16))
        return carry

    lax.fori_loop(0, nchunks - 1 if offset else nchunks, body, 0)
    if offset:
        last = (nchunks - 1) * rows
        dst[last:nrows - offset, :] = src[last + offset:, :].astype(BF16)
        dst[nrows - offset:, :] = tail[...].astype(BF16)


def _proj_kernel(*refs, transposed, skip_from, skip):
    j = pl.program_id(0)
    i = pl.program_id(1)
    x_ref, w_ref = refs[:2]
    tail_ref = refs[2] if skip_from is not None else None
    o_ref, wb_ref = refs[-2:]

    if skip_from is None:
        pl.when(i == 0)(lambda: _cast_rows(w_ref, wb_ref, 0, None))
    else:
        pl.when((i == 0) & (j < skip_from))(lambda: _cast_rows(w_ref, wb_ref, 0, None))
        pl.when((i == 0) & (j >= skip_from))(lambda: _cast_rows(w_ref, wb_ref, skip, tail_ref))

    contract = (((1,), (1,)), ((), ())) if transposed else (((1,), (0,)), ((), ()))
    o_ref[...] = lax.dot_general(x_ref[...], wb_ref[...], contract, preferred_element_type=F32)


def _proj(x, w, layer, n, col0=0, transposed=False, skip_from=None, skip=0, tm=1024, tn=1024):
    m, k = x.shape
    tm = min(tm, m)
    tn = min(tn, n)
    in_specs = [pl.BlockSpec((tm, k), lambda j, i: (i, 0))]
    if transposed:
        in_specs.append(pl.BlockSpec((None, tn, k), lambda j, i: (layer, col0 + j, 0)))
        scratch = [pltpu.VMEM((tn, k), BF16)]
    else:
        in_specs.append(pl.BlockSpec((None, k, tn), lambda j, i: (layer, 0, col0 + j)))
        scratch = [pltpu.VMEM((k, tn), BF16)]
    args = [x, w]
    if skip_from is not None:
        in_specs.append(pl.BlockSpec((None, skip, k),
                                     lambda j, i: (layer, (col0 + j + 1) * (tn // skip), 0)))
        args.append(w)
    return pl.pallas_call(
        functools.partial(_proj_kernel, transposed=transposed, skip_from=skip_from, skip=skip),
        grid=(n // tn, m // tm),
        in_specs=in_specs,
        out_specs=pl.BlockSpec((tm, tn), lambda j, i: (i, j)),
        out_shape=jax.ShapeDtypeStruct((m, n), F32),
        scratch_shapes=scratch,
        compiler_params=_params("parallel", "arbitrary"),
        name="proj",
    )(*args)


def _out_proj_kernel(*refs, n_pairs, emit_norm, nj):
    i = pl.program_id(0)
    j = pl.program_id(1)
    xs = refs[:n_pairs]
    ws = refs[n_pairs:2 * n_pairs]
    res_ref = refs[2 * n_pairs]
    pos = 2 * n_pairs + 1
    g_ref = refs[pos] if emit_norm else None
    pos += int(emit_norm)
    o_ref = refs[pos]
    xn_ref = refs[pos + 1] if emit_norm else None
    pos += 1 + int(emit_norm)
    wb = refs[pos:pos + n_pairs]
    rows_ref = refs[pos + n_pairs] if emit_norm else None

    @pl.when(i == 0)
    def _():
        for p in range(n_pairs):
            _cast_rows(ws[p], wb[p].at[j], 0, None)

    acc = None
    for p in range(n_pairs):
        d = jnp.dot(xs[p][...], wb[p][j], preferred_element_type=F32)
        acc = d if acc is None else acc + d
    acc = res_ref[...] + acc
    o_ref[...] = acc
    if emit_norm:
        rows_ref[j] = acc

        @pl.when(j == nj - 1)
        def _():
            tn = acc.shape[1]
            ssq = None
            for jj in range(nj):
                t = rows_ref[jj]
                s = jnp.sum(t * t, axis=-1, keepdims=True)
                ssq = s if ssq is None else ssq + s
            scale = lax.rsqrt(ssq * (1.0 / (nj * tn)) + EPS)
            for jj in range(nj):
                sl = slice(jj * tn, (jj + 1) * tn)
                xn_ref[:, sl] = (rows_ref[jj] * scale * g_ref[:, sl]).astype(xn_ref.dtype)


def _out_proj(xs, w, layer, row_blocks, res, next_norm_g=None, tm=512, tn=512):
    m, n = res.shape
    nj = n // tn
    emit_norm = next_norm_g is not None
    in_specs = [pl.BlockSpec((tm, x.shape[1]), lambda i, j: (i, 0)) for x in xs]
    in_specs += [pl.BlockSpec((None, x.shape[1], tn),
                              lambda i, j, rb=rb: (layer, rb, jnp.where(i == 0, j, nj - 1)))
                 for x, rb in zip(xs, row_blocks)]
    in_specs.append(pl.BlockSpec((tm, tn), lambda i, j: (i, j)))
    args = list(xs) + [w] * len(xs) + [res]
    out_specs = [pl.BlockSpec((tm, tn), lambda i, j: (i, j))]
    out_shape = [jax.ShapeDtypeStruct((m, n), F32)]
    scratch = [pltpu.VMEM((nj, x.shape[1], tn), BF16) for x in xs]
    if emit_norm:
        in_specs.append(pl.BlockSpec((1, n), lambda i, j: (0, 0)))
        args.append(next_norm_g.reshape(1, n))
        out_specs.append(pl.BlockSpec((tm, n), lambda i, j: (i, 0)))
        out_shape.append(jax.ShapeDtypeStruct((m, n), BF16))
        scratch.append(pltpu.VMEM((nj, tm, tn), F32))
    outs = pl.pallas_call(
        functools.partial(_out_proj_kernel, n_pairs=len(xs), emit_norm=emit_norm, nj=nj),
        grid=(m // tm, nj),
        in_specs=in_specs,
        out_specs=out_specs,
        out_shape=out_shape,
        scratch_shapes=scratch,
        compiler_params=_params("arbitrary", "arbitrary"),
        name="out_proj",
    )(*args)
    return (outs[0], outs[1]) if emit_norm else (outs[0], None)


def _gates_kernel(ba_ref, alog_ref, dt_ref, g_ref, gt_ref):
    x = ba_ref[...]
    col = lax.broadcasted_iota(jnp.int32, x.shape, 1)
    row = lax.broadcasted_iota(jnp.int32, x.shape, 0)
    beta = _sigmoid(x)
    g = -jnp.exp(alog_ref[...]) * _softplus(x + dt_ref[...])
    rin = row & (DN_CHUNK - 1)
    gc = g
    s = 1
    while s < DN_CHUNK:
        gc = gc + jnp.where(rin >= s, pltpu.roll(gc, s, axis=0), 0.0)
        s *= 2
    out = jnp.where(col < DN_HEADS, beta,
                    jnp.where(col < 2 * DN_HEADS, g, pltpu.roll(gc, DN_HEADS, axis=1)))
    g_ref[...] = out
    gt_ref[...] = out.T


def _gates(ba, a_log, dt_bias, ts=512):
    m = ba.shape[0]
    ts = min(ts, m)
    pad = lambda v: jnp.zeros((1, LANES), F32).at[0, DN_HEADS:2 * DN_HEADS].set(v.astype(F32))
    return pl.pallas_call(
        _gates_kernel,
        grid=(m // ts,),
        in_specs=[pl.BlockSpec((ts, LANES), lambda i: (i, 0)),
                  pl.BlockSpec((1, LANES), lambda i: (0, 0)),
                  pl.BlockSpec((1, LANES), lambda i: (0, 0))],
        out_specs=[pl.BlockSpec((ts, LANES), lambda i: (i, 0)),
                   pl.BlockSpec((LANES, ts), lambda i: (0, i))],
        out_shape=[jax.ShapeDtypeStruct((m, LANES), F32),
                   jax.ShapeDtypeStruct((LANES, m), F32)],
        compiler_params=_params("parallel"),
        name="dn_gates",
    )(ba, pad(a_log), pad(dt_bias))


def _dn_kernel(q_ref, k_ref, v_ref, hq_ref, hk_ref, hv_ref, z_ref, g_ref, gt_ref,
               wq_ref, wk_ref, wv_ref, ng_ref, o_ref, state_ref):
    i = pl.program_id(1)
    c = DN_CHUNK

    @pl.when(i == 0)
    def _():
        state_ref[...] = jnp.zeros_like(state_ref)

    def conv_act(cur_ref, halo_ref, w_ref):
        halo = jnp.where(i == 0, 0.0, halo_ref[...])
        x = jnp.concatenate([halo, cur_ref[...]], axis=0)
        w = w_ref[...]
        y = x * w[DN_CONV - 1:DN_CONV, :]
        for tap in range(DN_CONV - 1):
            y = y + pltpu.roll(x, DN_CONV - 1 - tap, axis=0) * w[tap:tap + 1, :]
        return _silu(y[SUBLANES:, :])

    q_all = conv_act(q_ref, hq_ref, wq_ref)
    k_all = conv_act(k_ref, hk_ref, wk_ref)
    v_all = conv_act(v_ref, hv_ref, wv_ref)
    ri = lax.broadcasted_iota(jnp.int32, (DN_HEADS, c, c), 1)
    ci = lax.broadcasted_iota(jnp.int32, (DN_HEADS, c, c), 2)
    heads = lambda t: jnp.stack([t[:, h * DN_HD:(h + 1) * DN_HD] for h in range(DN_HEADS)])

    def chunk(r0, state):
        q = heads(q_all[r0:r0 + c])
        k = heads(k_all[r0:r0 + c])
        v = heads(v_all[r0:r0 + c])
        q = q * lax.rsqrt(jnp.sum(q * q, axis=-1, keepdims=True) + EPS) * (DN_HD ** -0.5)
        k = k * lax.rsqrt(jnp.sum(k * k, axis=-1, keepdims=True) + EPS)
        gates = g_ref[r0:r0 + c, :]
        gates_t = gt_ref[:, r0:r0 + c]
        beta = jnp.stack([gates[:, h:h + 1] for h in range(DN_HEADS)])
        gcol = jnp.stack([gates[:, 2 * DN_HEADS + h:2 * DN_HEADS + h + 1]
                          for h in range(DN_HEADS)])
        grow = jnp.stack([gates_t[2 * DN_HEADS + h:2 * DN_HEADS + h + 1, :]
                          for h in range(DN_HEADS)])
        glast = grow[:, :, c - 1:c]
        decay = jnp.exp(jnp.minimum(gcol - grow, 0.0))
        kb = k * beta
        s2 = _bmm_nt(jnp.concatenate([kb, q], axis=1), k)
        a_mat = jnp.where(ri > ci, s2[:, :c] * decay, 0.0)
        qk = jnp.where(ri >= ci, s2[:, c:] * decay, 0.0)
        x = _bmm3(a_mat, a_mat)
        n = -a_mat
        p = 2
        while 2 * p < c:
            mm = _bmm3 if p <= DN_SPLIT_MAX_POWER else _bmm
            both = mm(x, jnp.concatenate([x, n], axis=-1))
            n = n + x + both[:, :, c:]
            x = both[:, :, :c]
            p *= 2
        n = n + x + _bmm(x, n)
        egc = jnp.exp(gcol)
        rhs = jnp.concatenate([v * beta, kb * egc], axis=-1)
        sol = rhs + _bmm(n, rhs)
        u_val, w_key = sol[:, :, :DN_HD], sol[:, :, DN_HD:]
        both = _bmm(jnp.concatenate([w_key, q * egc], axis=1), state)
        v_new = u_val - both[:, :c]
        kd_t = jnp.swapaxes(k * jnp.exp(glast - gcol), 1, 2)
        tail = _bmm(jnp.concatenate([qk, kd_t], axis=1), v_new)
        o = both[:, c:] + tail[:, :c]
        o = o * lax.rsqrt(jnp.mean(o * o, axis=-1, keepdims=True) + EPS) * ng_ref[...]
        return o, state * jnp.exp(glast) + tail[:, c:]

    state = state_ref[...]
    for r0 in range(0, q_ref.shape[0], c):
        o, state = chunk(r0, state)
        for h in range(DN_HEADS):
            sl = slice(h * DN_HD, (h + 1) * DN_HD)
            o_ref[r0:r0 + c, sl] = (o[h] * _silu(z_ref[r0:r0 + c, sl])).astype(o_ref.dtype)
    state_ref[...] = state


def _deltanet(p_main, gates, gates_t, conv_w, norm_g, bsz, seq):
    c = DN_CHUNK * DN_CHUNKS_PER_STEP
    ns = seq // c
    row = lambda b, i: b * ns + i
    halo_row = lambda b, i: jnp.maximum((b * ns + i) * (c // SUBLANES) - 1, 0)
    cur = lambda cb: pl.BlockSpec((c, DN_WIDTH), lambda b, i: (row(b, i), cb))
    halo = lambda cb: pl.BlockSpec((SUBLANES, DN_WIDTH), lambda b, i: (halo_row(b, i), cb))
    wspec = lambda cb: pl.BlockSpec((DN_CONV, DN_WIDTH), lambda b, i: (0, cb))
    return pl.pallas_call(
        _dn_kernel,
        grid=(bsz, ns),
        in_specs=[cur(0), cur(1), cur(2), halo(0), halo(1), halo(2), cur(3),
                  pl.BlockSpec((c, LANES), lambda b, i: (row(b, i), 0)),
                  pl.BlockSpec((LANES, c), lambda b, i: (0, row(b, i))),
                  wspec(0), wspec(1), wspec(2),
                  pl.BlockSpec((1, DN_HD), lambda b, i: (0, 0))],
        out_specs=pl.BlockSpec((c, DN_WIDTH), lambda b, i: (row(b, i), 0)),
        out_shape=jax.ShapeDtypeStruct((bsz * seq, DN_WIDTH), BF16),
        scratch_shapes=[pltpu.VMEM((DN_HEADS, DN_HD, DN_HD), F32)],
        compiler_params=_params("parallel", "arbitrary"),
        name="deltanet",
    )(p_main, p_main, p_main, p_main, p_main, p_main, p_main, gates, gates_t,
      conv_w, conv_w, conv_w, norm_g.reshape(1, DN_HD))


def _cv_kernel(a_ref, b_ref, ha_ref, hb_ref, z_ref, w_ref, bias_ref, lg_ref, lb_ref, o_ref):
    i = pl.program_id(1)
    ts = a_ref.shape[0]
    glu_cur = a_ref[...] * _sigmoid(b_ref[...])
    glu_halo = jnp.where(i == 0, 0.0, ha_ref[...] * _sigmoid(hb_ref[...]))
    x = jnp.concatenate([glu_halo, glu_cur], axis=0)
    w = w_ref[...]
    shifted = [x] + [pltpu.roll(x, r, axis=0) for r in range(1, SUBLANES)]
    acc = None
    for tap in range(CV_KERNEL):
        back = CV_KERNEL - 1 - tap
        start = CV_HALO - (back // SUBLANES) * SUBLANES
        term = shifted[back % SUBLANES][start:start + ts, :] * w[tap:tap + 1, :]
        acc = term if acc is None else acc + term
    y = acc + bias_ref[...]
    mu = jnp.mean(y, axis=-1, keepdims=True)
    yc = y - mu
    y = yc * lax.rsqrt(jnp.mean(yc * yc, axis=-1, keepdims=True) + EPS) * lg_ref[...] + lb_ref[...]
    o_ref[...] = (_silu(y) * _silu(z_ref[...])).astype(o_ref.dtype)


def _conformer(p_main, cb_a, cb_b, cb_z, dw_w, dw_b, ln_g, ln_b, bsz, seq, ts=256):
    ts = min(ts, seq)
    ns = seq // ts
    row = lambda b, i: b * ns + i
    halo_row = lambda b, i: jnp.maximum((b * ns + i) * (ts // CV_HALO) - 1, 0)
    cur = lambda cb: pl.BlockSpec((ts, CV_WIDTH), lambda b, i: (row(b, i), cb))
    halo = lambda cb: pl.BlockSpec((CV_HALO, CV_WIDTH), lambda b, i: (halo_row(b, i), cb))
    vec = pl.BlockSpec((1, CV_WIDTH), lambda b, i: (0, 0))
    return pl.pallas_call(
        _cv_kernel,
        grid=(bsz, ns),
        in_specs=[cur(cb_a), cur(cb_b), halo(cb_a), halo(cb_b), cur(cb_z),
                  pl.BlockSpec((CV_KERNEL, CV_WIDTH), lambda b, i: (0, 0)), vec, vec, vec],
        out_specs=pl.BlockSpec((ts, CV_WIDTH), lambda b, i: (row(b, i), 0)),
        out_shape=jax.ShapeDtypeStruct((bsz * seq, CV_WIDTH), BF16),
        compiler_params=_params("parallel", "parallel"),
        name="conformer_conv",
    )(p_main, p_main, p_main, p_main, p_main, dw_w,
      dw_b.reshape(1, CV_WIDTH), ln_g.reshape(1, CV_WIDTH), ln_b.reshape(1, CV_WIDTH))


def _mem_kernel(q_ref, z_ref, k_ref, v_ref, qg_ref, kg_ref, o_ref):
    for h in range(MEM_HEADS):
        sl = slice(h * MEM_HD, (h + 1) * MEM_HD)
        q = q_ref[:, sl]
        qn = q * lax.rsqrt(jnp.mean(q * q, axis=-1, keepdims=True) + EPS) * qg_ref[...]
        k = k_ref[:, sl]
        kn = k * lax.rsqrt(jnp.mean(k * k, axis=-1, keepdims=True) + EPS) * kg_ref[...]
        s = _mm_nt(qn * (MEM_HD ** -0.5 * LOG2E), kn)
        p = jnp.exp2(s - jnp.max(s, axis=-1, keepdims=True))
        o = _mm(p, v_ref[:, sl]) * (1.0 / jnp.sum(p, axis=-1, keepdims=True))
        o_ref[:, sl] = (o * _silu(z_ref[:, sl])).astype(o_ref.dtype)


def _mem_attend(p, cb_q, cb_z, mkv, q_norm_g, k_norm_g, bsz, seq, ts=512):
    ts = min(ts, seq)
    ns = seq // ts
    mlen = mkv.shape[0] // bsz
    vec = pl.BlockSpec((1, MEM_HD), lambda b, i: (0, 0))
    return pl.pallas_call(
        _mem_kernel,
        grid=(bsz, ns),
        in_specs=[pl.BlockSpec((ts, MEM_WIDTH), lambda b, i: (b * ns + i, cb_q)),
                  pl.BlockSpec((ts, MEM_WIDTH), lambda b, i: (b * ns + i, cb_z)),
                  pl.BlockSpec((mlen, MEM_WIDTH), lambda b, i: (b, 0)),
                  pl.BlockSpec((mlen, MEM_WIDTH), lambda b, i: (b, 1)),
                  vec, vec],
        out_specs=pl.BlockSpec((ts, MEM_WIDTH), lambda b, i: (b * ns + i, 0)),
        out_shape=jax.ShapeDtypeStruct((bsz * seq, MEM_WIDTH), BF16),
        compiler_params=_params("parallel", "parallel"),
        name="mem_attend",
    )(p, p, mkv, mkv, q_norm_g.reshape(1, MEM_HD), k_norm_g.reshape(1, MEM_HD))


def _sb_kernel(q_ref, k_ref, v_ref, z_ref, qg_ref, kg_ref, o_ref, kn_ref, vb_ref, *, nq):
    i = pl.program_id(2)
    tk = SB_TK
    tq = q_ref.shape[0]
    head_lanes = [slice(hh * SB_HD, (hh + 1) * SB_HD) for hh in range(SB_HEADS_PER_STEP)]

    @pl.when(i == 0)
    def _():
        for hl in head_lanes:
            k = k_ref[:, hl]
            kn = k * lax.rsqrt(jnp.mean(k * k, axis=-1, keepdims=True) + EPS) * kg_ref[...]
            kn_ref[:, hl] = kn.astype(BF16)
        vb_ref[...] = v_ref[...].astype(BF16)

    qns = []
    for hl in head_lanes:
        q = q_ref[:, hl]
        qn = q * lax.rsqrt(jnp.mean(q * q, axis=-1, keepdims=True) + EPS) * qg_ref[...]
        qns.append((qn * (SB_HD ** -0.5 * LOG2E)).astype(BF16))
    nsub = tq // tk

    uj = lax.broadcasted_iota(jnp.int32, (2 * tk, tk + LANES), 0) & (tk - 1)
    us = lax.broadcasted_iota(jnp.int32, (2 * tk, tk + LANES), 1)
    u_aug = jnp.where((uj >= us) | (us >= tk), 1.0, 0.0).astype(BF16)

    sign_bit = jnp.uint32(0x80000000)

    def sub_block(z, vis):
        neg_abs = lax.bitcast_convert_type(lax.bitcast_convert_type(z, jnp.uint32) | sign_bit, F32)
        cost = jnp.maximum(z, 0.0) + jnp.log2(1.0 + jnp.exp2(neg_abs))
        if vis is not None:
            cost = jnp.where(vis, cost, 0.0)
        hi = cost.astype(BF16)
        lo = (cost - hi.astype(F32)).astype(BF16)
        inc = jnp.dot(jnp.concatenate([hi, lo], axis=1), u_aug, preferred_element_type=F32)
        return z - inc[:, :tk], inc[:, tk:]

    def key_span(hh, sub_lo, sub_hi, q_sub0, run):
        rows = pl.ds(sub_lo * tk, (sub_hi - sub_lo) * tk)
        kb = kn_ref[rows, head_lanes[hh]]
        vb = vb_ref[rows, head_lanes[hh]]
        z_all = lax.dot_general(qns[hh], kb, (((1,), (1,)), ((), ())),
                                preferred_element_type=F32)
        ws = [None] * (sub_hi - sub_lo)
        for s in range(sub_hi - 1, sub_lo - 1, -1):
            vis = None
            if s >= q_sub0:
                vis = (lax.broadcasted_iota(jnp.int32, (tq, tk), 1) + (s - q_sub0) * tk
                       < lax.broadcasted_iota(jnp.int32, (tq, tk), 0))
            pre, tot = sub_block(z_all[:, (s - sub_lo) * tk:(s - sub_lo + 1) * tk], vis)
            w = jnp.exp2(pre + run)
            if vis is not None:
                w = jnp.where(vis, w, 0.0)
            ws[s - sub_lo] = w.astype(BF16)
            run = run - tot
        return jnp.dot(jnp.concatenate(ws, axis=1), vb, preferred_element_type=F32), run

    def query_block(qi):
        nh = len(head_lanes)
        q_sub0 = qi * nsub
        runs = [jnp.zeros((tq, LANES), F32) for _ in range(nh)]
        accs = [jnp.zeros((tq, SB_HD), F32) for _ in range(nh)]

        def sweep(sub_hi, sub_stop, accs, runs):
            accs, runs = list(accs), list(runs)
            while sub_hi > sub_stop:
                sub_lo = max(sub_hi - nsub, sub_stop)
                for hh in range(nh):
                    pv, runs[hh] = key_span(hh, sub_lo, sub_hi, q_sub0, runs[hh])
                    accs[hh] = accs[hh] + pv
                sub_hi = sub_lo
            return accs, runs

        always_from = max(q_sub0 - SB_ALWAYS_SUBS, 0)
        accs, runs = sweep(q_sub0 + nsub, always_from, accs, runs)
        if always_from > 0:
            live = functools.reduce(jnp.maximum, [jnp.max(r) for r in runs]) >= SB_UNDERFLOW_LOG2
            rest = lambda carry: tuple(sum(sweep(always_from, 0, carry[:nh], carry[nh:]), []))
            carry = lax.cond(live, rest, lambda carry: carry, tuple(accs) + tuple(runs))
            accs = carry[:nh]
        for hh, hl in enumerate(head_lanes):
            o_ref[:, hl] = (accs[hh] * _silu(z_ref[:, hl])).astype(o_ref.dtype)

    for qi in range(nq):
        pl.when(i == qi)(functools.partial(query_block, qi))


def _stickbreak(p, q_norm_g, k_norm_g, bsz, seq):
    tq = min(SB_TQ, seq)
    nq = seq // tq
    ng = SB_HEADS // SB_HEADS_PER_STEP
    gw = SB_HEADS_PER_STEP * SB_HD
    vec = pl.BlockSpec((1, SB_HD), lambda b, h, i: (0, 0))
    return pl.pallas_call(
        functools.partial(_sb_kernel, nq=nq),
        grid=(bsz, ng, nq),
        in_specs=[pl.BlockSpec((tq, gw), lambda b, h, i: (b * nq + i, h)),
                  pl.BlockSpec((seq, gw), lambda b, h, i: (b, ng + h)),
                  pl.BlockSpec((seq, gw), lambda b, h, i: (b, 2 * ng + h)),
                  pl.BlockSpec((tq, gw), lambda b, h, i: (b * nq + i, 3 * ng + h)),
                  vec, vec],
        out_specs=pl.BlockSpec((tq, gw), lambda b, h, i: (b * nq + i, h)),
        out_shape=jax.ShapeDtypeStruct((bsz * seq, SB_WIDTH), BF16),
        scratch_shapes=[pltpu.VMEM((seq, gw), BF16), pltpu.VMEM((seq, gw), BF16)],
        compiler_params=_params("parallel", "parallel", "arbitrary"),
        name="stickbreak",
    )(p, p, p, p, q_norm_g.reshape(1, SB_HD), k_norm_g.reshape(1, SB_HD))


def _even_layer(h, xn, mkv, k_norm_g, j, w_in, conv_qkv, a_log, dt_bias, dn_norm_g, dw_w, dw_b,
                ln_g, ln_b, q_norm_m, w_out, next_norm_g, bsz, seq):
    n_main = w_in.shape[2] - 2 * DN_HEADS
    w_in_t = jnp.swapaxes(w_in, 1, 2)
    p_main = _proj(xn, w_in_t, j, n_main, transposed=True,
                   skip_from=4 * DN_WIDTH // 1024, skip=2 * DN_HEADS, tn=1024)
    ba = _proj(xn, w_in_t, j, LANES, col0=4 * DN_WIDTH // LANES, transposed=True, tn=LANES)
    gates, gates_t = _gates(ba, a_log, dt_bias)
    o_a = _deltanet(p_main, gates, gates_t, conv_qkv, dn_norm_g, bsz, seq)
    o_b = _conformer(p_main, 4, 5, 6, dw_w, dw_b, ln_g, ln_b, bsz, seq)
    o_m = _mem_attend(p_main, 7, 8, mkv, q_norm_m, k_norm_g, bsz, seq)
    return _out_proj([o_a, o_b, o_m], w_out, j, [0, 1, 2], h, next_norm_g)


def _odd_layer(h, xn, mkv, k_norm_g, j, w_in, q_norm_c, k_norm_c, q_norm_m, w_out, next_norm_g,
               bsz, seq):
    p = _proj(xn, w_in, j, w_in.shape[2])
    o_c = _stickbreak(p, q_norm_c, k_norm_c, bsz, seq)
    o_m = _mem_attend(p, 8, 9, mkv, q_norm_m, k_norm_g, bsz, seq)
    return _out_proj([o_c, o_m], w_out, j, [0, SB_WIDTH // MEM_WIDTH], h, next_norm_g)


def kernel(x, mem, mem_norm_g, w_mem_kv, mem_k_norm_g, ev_norm_g, ev_w_in, ev_conv_qkv, ev_a_log,
           ev_dt_bias, ev_dn_norm_g, ev_dw_w, ev_dw_b, ev_ln_g, ev_ln_b, ev_q_norm_m, ev_w_out,
           od_norm_g, od_w_in, od_q_norm_c, od_k_norm_c, od_q_norm_m, od_w_out):
    bsz, seq, d = x.shape
    mlen = mem.shape[1]
    depth = ev_norm_g.shape[0] + od_norm_g.shape[0]
    mkv = _proj(_rmsnorm_bf16(mem.reshape(bsz * mlen, d), mem_norm_g, tm=256),
                w_mem_kv[None], 0, w_mem_kv.shape[1])
    h = x.reshape(bsz * seq, d)
    layer_gain = lambda layer: (ev_norm_g if layer % 2 == 0 else od_norm_g)[layer // 2]
    xn = _rmsnorm_bf16(h, layer_gain(0))
    for layer in range(depth):
        j = layer // 2
        next_g = layer_gain(layer + 1) if layer + 1 < depth else None
        if layer % 2 == 0:
            h, xn = _even_layer(h, xn, mkv, mem_k_norm_g, j, ev_w_in, ev_conv_qkv[j], ev_a_log[j],
                                ev_dt_bias[j], ev_dn_norm_g[j], ev_dw_w[j], ev_dw_b[j], ev_ln_g[j],
                                ev_ln_b[j], ev_q_norm_m[j], ev_w_out, next_g, bsz, seq)
        else:
            h, xn = _odd_layer(h, xn, mkv, mem_k_norm_g, j, od_w_in, od_q_norm_c[j], od_k_norm_c[j],
                               od_q_norm_m[j], od_w_out, next_g, bsz, seq)
    return h.reshape(bsz, seq, d)
```

```python
import functools

import jax
import jax.numpy as jnp
from jax import lax
from jax.experimental import pallas as pl
from jax.experimental.pallas import tpu as pltpu

F32 = jnp.float32
BF16 = jnp.bfloat16
EPS = 1e-6
LOG2E = 1.4426950408889634

LANES = 128
SUBLANES = 8
VMEM_LIMIT = 48 * 1024 * 1024

DN_HEADS = 8
DN_HD = 128
DN_WIDTH = DN_HEADS * DN_HD
DN_CONV = 4
DN_CHUNK = 128
DN_CHUNKS_PER_STEP = 2
DN_SPLIT_MAX_POWER = 8
CV_WIDTH = 1024
CV_KERNEL = 31
CV_HALO = 32
SB_HEADS = 16
SB_HD = 128
SB_WIDTH = SB_HEADS * SB_HD
SB_HEADS_PER_STEP = 4
SB_TQ = 512
SB_TK = 128
SB_UNDERFLOW_LOG2 = -160.0
SB_ALWAYS_SUBS = 2
MEM_HEADS = 4
MEM_HD = 256
MEM_WIDTH = MEM_HEADS * MEM_HD


def _params(*sem):
    return pltpu.CompilerParams(dimension_semantics=sem, vmem_limit_bytes=VMEM_LIMIT)


def _sigmoid(x):
    return 1.0 / (1.0 + jnp.exp(-x))


def _silu(x):
    return x * _sigmoid(x)


def _softplus(x):
    return jnp.maximum(x, 0.0) + jnp.log(1.0 + jnp.exp(-jnp.abs(x)))


def _mm(a, b):
    return jnp.dot(a.astype(BF16), b.astype(BF16), preferred_element_type=F32)


def _mm_nt(a, b):
    return lax.dot_general(a.astype(BF16), b.astype(BF16), (((1,), (1,)), ((), ())),
                           preferred_element_type=F32)


def _bmm(a, b):
    return lax.dot_general(a.astype(BF16), b.astype(BF16), (((2,), (1,)), ((0,), (0,))),
                           preferred_element_type=F32)


def _bmm3(a, b):
    a_hi = a.astype(BF16)
    b_hi = b.astype(BF16)
    a_lo = (a - a_hi.astype(F32)).astype(BF16)
    b_lo = (b - b_hi.astype(F32)).astype(BF16)
    lhs = jnp.concatenate([a_hi, a_hi, a_lo], axis=2)
    rhs = jnp.concatenate([b_hi, b_lo, b_hi], axis=1)
    return lax.dot_general(lhs, rhs, (((2,), (1,)), ((0,), (0,))), preferred_element_type=F32)


def _bmm_nt(a, b):
    return lax.dot_general(a.astype(BF16), b.astype(BF16), (((2,), (2,)), ((0,), (0,))),
                           preferred_element_type=F32)


def _rmsnorm_kernel(x_ref, g_ref, o_ref):
    x = x_ref[...]
    ms = jnp.mean(x * x, axis=-1, keepdims=True)
    o_ref[...] = (x * lax.rsqrt(ms + EPS) * g_ref[...]).astype(o_ref.dtype)


def _rmsnorm_bf16(x2d, g, tm=512):
    m, d = x2d.shape
    return pl.pallas_call(
        _rmsnorm_kernel,
        grid=(m // tm,),
        in_specs=[pl.BlockSpec((tm, d), lambda i: (i, 0)),
                  pl.BlockSpec((1, d), lambda i: (0, 0))],
        out_specs=pl.BlockSpec((tm, d), lambda i: (i, 0)),
        out_shape=jax.ShapeDtypeStruct((m, d), BF16),
        compiler_params=_params("parallel"),
        name="rmsnorm",
    )(x2d, g.reshape(1, d))


PROJ_CAST_ROWS = 256


def _cast_rows(src, dst, offset, tail):
    nrows = src.shape[0]
    rows = min(PROJ_CAST_ROWS, nrows)
    nchunks = nrows // rows

    def body(r, carry):
        start = r * rows
        dst[pl.ds(pl.multiple_of(start, rows), rows), :] = (
            src[pl.ds(pl.multiple_of(start + offset, SUBLANES), rows), :].astype(BF16))
        return carry

    lax.fori_loop(0, nchunks - 1 if offset else nchunks, body, 0)
    if offset:
        last = (nchunks - 1) * rows
        dst[last:nrows - offset, :] = src[last + offset:, :].astype(BF16)
        dst[nrows - offset:, :] = tail[...].astype(BF16)


def _proj_kernel(*refs, transposed, skip_from, skip):
    j = pl.program_id(0)
    i = pl.program_id(1)
    x_ref, w_ref = refs[:2]
    tail_ref = refs[2] if skip_from is not None else None
    o_ref, wb_ref = refs[-2:]

    if skip_from is None:
        pl.when(i == 0)(lambda: _cast_rows(w_ref, wb_ref, 0, None))
    else:
        pl.when((i == 0) & (j < skip_from))(lambda: _cast_rows(w_ref, wb_ref, 0, None))
        pl.when((i == 0) & (j >= skip_from))(lambda: _cast_rows(w_ref, wb_ref, skip, tail_ref))

    contract = (((1,), (1,)), ((), ())) if transposed else (((1,), (0,)), ((), ()))
    o_ref[...] = lax.dot_general(x_ref[...], wb_ref[...], contract, preferred_element_type=F32)


def _proj(x, w, layer, n, col0=0, transposed=False, skip_from=None, skip=0, tm=1024, tn=1024):
    m, k = x.shape
    tm = min(tm, m)
    tn = min(tn, n)
    in_specs = [pl.BlockSpec((tm, k), lambda j, i: (i, 0))]
    if transposed:
        in_specs.append(pl.BlockSpec((None, tn, k), lambda j, i: (layer, col0 + j, 0)))
        scratch = [pltpu.VMEM((tn, k), BF16)]
    else:
        in_specs.append(pl.BlockSpec((None, k, tn), lambda j, i: (layer, 0, col0 + j)))
        scratch = [pltpu.VMEM((k, tn), BF16)]
    args = [x, w]
    if skip_from is not None:
        in_specs.append(pl.BlockSpec((None, skip, k),
                                     lambda j, i: (layer, (col0 + j + 1) * (tn // skip), 0)))
        args.append(w)
    return pl.pallas_call(
        functools.partial(_proj_kernel, transposed=transposed, skip_from=skip_from, skip=skip),
        grid=(n // tn, m // tm),
        in_specs=in_specs,
        out_specs=pl.BlockSpec((tm, tn), lambda j, i: (i, j)),
        out_shape=jax.ShapeDtypeStruct((m, n), F32),
        scratch_shapes=scratch,
        compiler_params=_params("parallel", "arbitrary"),
        name="proj",
    )(*args)


def _out_proj_kernel(*refs, n_pairs, emit_norm, nj):
    i = pl.program_id(0)
    j = pl.program_id(1)
    xs = refs[:n_pairs]
    ws = refs[n_pairs:2 * n_pairs]
    res_ref = refs[2 * n_pairs]
    pos = 2 * n_pairs + 1
    g_ref = refs[pos] if emit_norm else None
    pos += int(emit_norm)
    o_ref = refs[pos]
    xn_ref = refs[pos + 1] if emit_norm else None
    pos += 1 + int(emit_norm)
    wb = refs[pos:pos + n_pairs]
    rows_ref = refs[pos + n_pairs] if emit_norm else None

    @pl.when(i == 0)
    def _():
        for p in range(n_pairs):
            _cast_rows(ws[p], wb[p].at[j], 0, None)

    acc = None
    for p in range(n_pairs):
        d = jnp.dot(xs[p][...], wb[p][j], preferred_element_type=F32)
        acc = d if acc is None else acc + d
    acc = res_ref[...] + acc
    o_ref[...] = acc
    if emit_norm:
        rows_ref[j] = acc

        @pl.when(j == nj - 1)
        def _():
            tn = acc.shape[1]
            ssq = None
            for jj in range(nj):
                t = rows_ref[jj]
                s = jnp.sum(t * t, axis=-1, keepdims=True)
                ssq = s if ssq is None else ssq + s
            scale = lax.rsqrt(ssq * (1.0 / (nj * tn)) + EPS)
            for jj in range(nj):
                sl = slice(jj * tn, (jj + 1) * tn)
                xn_ref[:, sl] = (rows_ref[jj] * scale * g_ref[:, sl]).astype(xn_ref.dtype)


def _out_proj(xs, w, layer, row_blocks, res, next_norm_g=None, tm=512, tn=512):
    m, n = res.shape
    nj = n // tn
    emit_norm = next_norm_g is not None
    in_specs = [pl.BlockSpec((tm, x.shape[1]), lambda i, j: (i, 0)) for x in xs]
    in_specs += [pl.BlockSpec((None, x.shape[1], tn),
                              lambda i, j, rb=rb: (layer, rb, jnp.where(i == 0, j, nj - 1)))
                 for x, rb in zip(xs, row_blocks)]
    in_specs.append(pl.BlockSpec((tm, tn), lambda i, j: (i, j)))
    args = list(xs) + [w] * len(xs) + [res]
    out_specs = [pl.BlockSpec((tm, tn), lambda i, j: (i, j))]
    out_shape = [jax.ShapeDtypeStruct((m, n), F32)]
    scratch = [pltpu.VMEM((nj, x.shape[1], tn), BF16) for x in xs]
    if emit_norm:
        in_specs.append(pl.BlockSpec((1, n), lambda i, j: (0, 0)))
        args.append(next_norm_g.reshape(1, n))
        out_specs.append(pl.BlockSpec((tm, n), lambda i, j: (i, 0)))
        out_shape.append(jax.ShapeDtypeStruct((m, n), BF16))
        scratch.append(pltpu.VMEM((nj, tm, tn), F32))
    outs = pl.pallas_call(
        functools.partial(_out_proj_kernel, n_pairs=len(xs), emit_norm=emit_norm, nj=nj),
        grid=(m // tm, nj),
        in_specs=in_specs,
        out_specs=out_specs,
        out_shape=out_shape,
        scratch_shapes=scratch,
        compiler_params=_params("arbitrary", "arbitrary"),
        name="out_proj",
    )(*args)
    return (outs[0], outs[1]) if emit_norm else (outs[0], None)


def _gates_kernel(ba_ref, alog_ref, dt_ref, g_ref, gt_ref):
    x = ba_ref[...]
    col = lax.broadcasted_iota(jnp.int32, x.shape, 1)
    row = lax.broadcasted_iota(jnp.int32, x.shape, 0)
    beta = _sigmoid(x)
    g = -jnp.exp(alog_ref[...]) * _softplus(x + dt_ref[...])
    rin = row & (DN_CHUNK - 1)
    gc = g
    s = 1
    while s < DN_CHUNK:
        gc = gc + jnp.where(rin >= s, pltpu.roll(gc, s, axis=0), 0.0)
        s *= 2
    out = jnp.where(col < DN_HEADS, beta,
                    jnp.where(col < 2 * DN_HEADS, g, pltpu.roll(gc, DN_HEADS, axis=1)))
    g_ref[...] = out
    gt_ref[...] = out.T


def _gates(ba, a_log, dt_bias, ts=512):
    m = ba.shape[0]
    ts = min(ts, m)
    pad = lambda v: jnp.zeros((1, LANES), F32).at[0, DN_HEADS:2 * DN_HEADS].set(v.astype(F32))
    return pl.pallas_call(
        _gates_kernel,
        grid=(m // ts,),
        in_specs=[pl.BlockSpec((ts, LANES), lambda i: (i, 0)),
                  pl.BlockSpec((1, LANES), lambda i: (0, 0)),
                  pl.BlockSpec((1, LANES), lambda i: (0, 0))],
        out_specs=[pl.BlockSpec((ts, LANES), lambda i: (i, 0)),
                   pl.BlockSpec((LANES, ts), lambda i: (0, i))],
        out_shape=[jax.ShapeDtypeStruct((m, LANES), F32),
                   jax.ShapeDtypeStruct((LANES, m), F32)],
        compiler_params=_params("parallel"),
        name="dn_gates",
    )(ba, pad(a_log), pad(dt_bias))


def _dn_kernel(q_ref, k_ref, v_ref, hq_ref, hk_ref, hv_ref, z_ref, g_ref, gt_ref,
               wq_ref, wk_ref, wv_ref, ng_ref, o_ref, state_ref):
    i = pl.program_id(1)
    c = DN_CHUNK

    @pl.when(i == 0)
    def _():
        state_ref[...] = jnp.zeros_like(state_ref)

    def conv_act(cur_ref, halo_ref, w_ref):
        halo = jnp.where(i == 0, 0.0, halo_ref[...])
        x = jnp.concatenate([halo, cur_ref[...]], axis=0)
        w = w_ref[...]
        y = x * w[DN_CONV - 1:DN_CONV, :]
        for tap in range(DN_CONV - 1):
            y = y + pltpu.roll(x, DN_CONV - 1 - tap, axis=0) * w[tap:tap + 1, :]
        return _silu(y[SUBLANES:, :])

    q_all = conv_act(q_ref, hq_ref, wq_ref)
    k_all = conv_act(k_ref, hk_ref, wk_ref)
    v_all = conv_act(v_ref, hv_ref, wv_ref)
    ri = lax.broadcasted_iota(jnp.int32, (DN_HEADS, c, c), 1)
    ci = lax.broadcasted_iota(jnp.int32, (DN_HEADS, c, c), 2)
    heads = lambda t: jnp.stack([t[:, h * DN_HD:(h + 1) * DN_HD] for h in range(DN_HEADS)])

    def chunk(r0, state):
        q = heads(q_all[r0:r0 + c])
        k = heads(k_all[r0:r0 + c])
        v = heads(v_all[r0:r0 + c])
        q = q * lax.rsqrt(jnp.sum(q * q, axis=-1, keepdims=True) + EPS) * (DN_HD ** -0.5)
        k = k * lax.rsqrt(jnp.sum(k * k, axis=-1, keepdims=True) + EPS)
        gates = g_ref[r0:r0 + c, :]
        gates_t = gt_ref[:, r0:r0 + c]
        beta = jnp.stack([gates[:, h:h + 1] for h in range(DN_HEADS)])
        gcol = jnp.stack([gates[:, 2 * DN_HEADS + h:2 * DN_HEADS + h + 1]
                          for h in range(DN_HEADS)])
        grow = jnp.stack([gates_t[2 * DN_HEADS + h:2 * DN_HEADS + h + 1, :]
                          for h in range(DN_HEADS)])
        glast = grow[:, :, c - 1:c]
        decay = jnp.exp(jnp.minimum(gcol - grow, 0.0))
        kb = k * beta
        s2 = _bmm_nt(jnp.concatenate([kb, q], axis=1), k)
        a_mat = jnp.where(ri > ci, s2[:, :c] * decay, 0.0)
        qk = jnp.where(ri >= ci, s2[:, c:] * decay, 0.0)
        x = _bmm3(a_mat, a_mat)
        n = -a_mat
        p = 2
        while 2 * p < c:
            mm = _bmm3 if p <= DN_SPLIT_MAX_POWER else _bmm
            both = mm(x, jnp.concatenate([x, n], axis=-1))
            n = n + x + both[:, :, c:]
            x = both[:, :, :c]
            p *= 2
        n = n + x + _bmm(x, n)
        egc = jnp.exp(gcol)
        rhs = jnp.concatenate([v * beta, kb * egc], axis=-1)
        sol = rhs + _bmm(n, rhs)
        u_val, w_key = sol[:, :, :DN_HD], sol[:, :, DN_HD:]
        both = _bmm(jnp.concatenate([w_key, q * egc], axis=1), state)
        v_new = u_val - both[:, :c]
        kd_t = jnp.swapaxes(k * jnp.exp(glast - gcol), 1, 2)
        tail = _bmm(jnp.concatenate([qk, kd_t], axis=1), v_new)
        o = both[:, c:] + tail[:, :c]
        o = o * lax.rsqrt(jnp.mean(o * o, axis=-1, keepdims=True) + EPS) * ng_ref[...]
        return o, state * jnp.exp(glast) + tail[:, c:]

    state = state_ref[...]
    for r0 in range(0, q_ref.shape[0], c):
        o, state = chunk(r0, state)
        for h in range(DN_HEADS):
            sl = slice(h * DN_HD, (h + 1) * DN_HD)
            o_ref[r0:r0 + c, sl] = (o[h] * _silu(z_ref[r0:r0 + c, sl])).astype(o_ref.dtype)
    state_ref[...] = state


def _deltanet(p_main, gates, gates_t, conv_w, norm_g, bsz, seq):
    c = DN_CHUNK * DN_CHUNKS_PER_STEP
    ns = seq // c
    row = lambda b, i: b * ns + i
    halo_row = lambda b, i: jnp.maximum((b * ns + i) * (c // SUBLANES) - 1, 0)
    cur = lambda cb: pl.BlockSpec((c, DN_WIDTH), lambda b, i: (row(b, i), cb))
    halo = lambda cb: pl.BlockSpec((SUBLANES, DN_WIDTH), lambda b, i: (halo_row(b, i), cb))
    wspec = lambda cb: pl.BlockSpec((DN_CONV, DN_WIDTH), lambda b, i: (0, cb))
    return pl.pallas_call(
        _dn_kernel,
        grid=(bsz, ns),
        in_specs=[cur(0), cur(1), cur(2), halo(0), halo(1), halo(2), cur(3),
                  pl.BlockSpec((c, LANES), lambda b, i: (row(b, i), 0)),
                  pl.BlockSpec((LANES, c), lambda b, i: (0, row(b, i))),
                  wspec(0), wspec(1), wspec(2),
                  pl.BlockSpec((1, DN_HD), lambda b, i: (0, 0))],
        out_specs=pl.BlockSpec((c, DN_WIDTH), lambda b, i: (row(b, i), 0)),
        out_shape=jax.ShapeDtypeStruct((bsz * seq, DN_WIDTH), BF16),
        scratch_shapes=[pltpu.VMEM((DN_HEADS, DN_HD, DN_HD), F32)],
        compiler_params=_params("parallel", "arbitrary"),
        name="deltanet",
    )(p_main, p_main, p_main, p_main, p_main, p_main, p_main, gates, gates_t,
      conv_w, conv_w, conv_w, norm_g.reshape(1, DN_HD))


def _cv_kernel(a_ref, b_ref, ha_ref, hb_ref, z_ref, w_ref, bias_ref, lg_ref, lb_ref, o_ref):
    i = pl.program_id(1)
    ts = a_ref.shape[0]
    glu_cur = a_ref[...] * _sigmoid(b_ref[...])
    glu_halo = jnp.where(i == 0, 0.0, ha_ref[...] * _sigmoid(hb_ref[...]))
    x = jnp.concatenate([glu_halo, glu_cur], axis=0)
    w = w_ref[...]
    shifted = [x] + [pltpu.roll(x, r, axis=0) for r in range(1, SUBLANES)]
    acc = None
    for tap in range(CV_KERNEL):
        back = CV_KERNEL - 1 - tap
        start = CV_HALO - (back // SUBLANES) * SUBLANES
        term = shifted[back % SUBLANES][start:start + ts, :] * w[tap:tap + 1, :]
        acc = term if acc is None else acc + term
    y = acc + bias_ref[...]
    mu = jnp.mean(y, axis=-1, keepdims=True)
    yc = y - mu
    y = yc * lax.rsqrt(jnp.mean(yc * yc, axis=-1, keepdims=True) + EPS) * lg_ref[...] + lb_ref[...]
    o_ref[...] = (_silu(y) * _silu(z_ref[...])).astype(o_ref.dtype)


def _conformer(p_main, cb_a, cb_b, cb_z, dw_w, dw_b, ln_g, ln_b, bsz, seq, ts=256):
    ts = min(ts, seq)
    ns = seq // ts
    row = lambda b, i: b * ns + i
    halo_row = lambda b, i: jnp.maximum((b * ns + i) * (ts // CV_HALO) - 1, 0)
    cur = lambda cb: pl.BlockSpec((ts, CV_WIDTH), lambda b, i: (row(b, i), cb))
    halo = lambda cb: pl.BlockSpec((CV_HALO, CV_WIDTH), lambda b, i: (halo_row(b, i), cb))
    vec = pl.BlockSpec((1, CV_WIDTH), lambda b, i: (0, 0))
    return pl.pallas_call(
        _cv_kernel,
        grid=(bsz, ns),
        in_specs=[cur(cb_a), cur(cb_b), halo(cb_a), halo(cb_b), cur(cb_z),
                  pl.BlockSpec((CV_KERNEL, CV_WIDTH), lambda b, i: (0, 0)), vec, vec, vec],
        out_specs=pl.BlockSpec((ts, CV_WIDTH), lambda b, i: (row(b, i), 0)),
        out_shape=jax.ShapeDtypeStruct((bsz * seq, CV_WIDTH), BF16),
        compiler_params=_params("parallel", "parallel"),
        name="conformer_conv",
    )(p_main, p_main, p_main, p_main, p_main, dw_w,
      dw_b.reshape(1, CV_WIDTH), ln_g.reshape(1, CV_WIDTH), ln_b.reshape(1, CV_WIDTH))


def _mem_kernel(q_ref, z_ref, k_ref, v_ref, qg_ref, kg_ref, o_ref):
    for h in range(MEM_HEADS):
        sl = slice(h * MEM_HD, (h + 1) * MEM_HD)
        q = q_ref[:, sl]
        qn = q * lax.rsqrt(jnp.mean(q * q, axis=-1, keepdims=True) + EPS) * qg_ref[...]
        k = k_ref[:, sl]
        kn = k * lax.rsqrt(jnp.mean(k * k, axis=-1, keepdims=True) + EPS) * kg_ref[...]
        s = _mm_nt(qn * (MEM_HD ** -0.5 * LOG2E), kn)
        p = jnp.exp2(s - jnp.max(s, axis=-1, keepdims=True))
        o = _mm(p, v_ref[:, sl]) * (1.0 / jnp.sum(p, axis=-1, keepdims=True))
        o_ref[:, sl] = (o * _silu(z_ref[:, sl])).astype(o_ref.dtype)


def _mem_attend(p, cb_q, cb_z, mkv, q_norm_g, k_norm_g, bsz, seq, ts=512):
    ts = min(ts, seq)
    ns = seq // ts
    mlen = mkv.shape[0] // bsz
    vec = pl.BlockSpec((1, MEM_HD), lambda b, i: (0, 0))
    return pl.pallas_call(
        _mem_kernel,
        grid=(bsz, ns),
        in_specs=[pl.BlockSpec((ts, MEM_WIDTH), lambda b, i: (b * ns + i, cb_q)),
                  pl.BlockSpec((ts, MEM_WIDTH), lambda b, i: (b * ns + i, cb_z)),
                  pl.BlockSpec((mlen, MEM_WIDTH), lambda b, i: (b, 0)),
                  pl.BlockSpec((mlen, MEM_WIDTH), lambda b, i: (b, 1)),
                  vec, vec],
        out_specs=pl.BlockSpec((ts, MEM_WIDTH), lambda b, i: (b * ns + i, 0)),
        out_shape=jax.ShapeDtypeStruct((bsz * seq, MEM_WIDTH), BF16),
        compiler_params=_params("parallel", "parallel"),
        name="mem_attend",
    )(p, p, mkv, mkv, q_norm_g.reshape(1, MEM_HD), k_norm_g.reshape(1, MEM_HD))


def _sb_kernel(q_ref, k_ref, v_ref, z_ref, qg_ref, kg_ref, o_ref, kn_ref, vb_ref, *, nq):
    i = pl.program_id(2)
    tk = SB_TK
    tq = q_ref.shape[0]
    head_lanes = [slice(hh * SB_HD, (hh + 1) * SB_HD) for hh in range(SB_HEADS_PER_STEP)]

    @pl.when(i == 0)
    def _():
        for hl in head_lanes:
            k = k_ref[:, hl]
            kn = k * lax.rsqrt(jnp.mean(k * k, axis=-1, keepdims=True) + EPS) * kg_ref[...]
            kn_ref[:, hl] = kn.astype(BF16)
        vb_ref[...] = v_ref[...].astype(BF16)

    qns = []
    for hl in head_lanes:
        q = q_ref[:, hl]
        qn = q * lax.rsqrt(jnp.mean(q * q, axis=-1, keepdims=True) + EPS) * qg_ref[...]
        qns.append((qn * (SB_HD ** -0.5 * LOG2E)).astype(BF16))
    nsub = tq // tk

    uj = lax.broadcasted_iota(jnp.int32, (2 * tk, tk + LANES), 0) & (tk - 1)
    us = lax.broadcasted_iota(jnp.int32, (2 * tk, tk + LANES), 1)
    u_aug = jnp.where((uj >= us) | (us >= tk), 1.0, 0.0).astype(BF16)

    sign_bit = jnp.uint32(0x80000000)

    def sub_block(z, vis):
        neg_abs = lax.bitcast_convert_type(lax.bitcast_convert_type(z, jnp.uint32) | sign_bit, F32)
        cost = jnp.maximum(z, 0.0) + jnp.log2(1.0 + jnp.exp2(neg_abs))
        if vis is not None:
            cost = jnp.where(vis, cost, 0.0)
        hi = cost.astype(BF16)
        lo = (cost - hi.astype(F32)).astype(BF16)
        inc = jnp.dot(jnp.concatenate([hi, lo], axis=1), u_aug, preferred_element_type=F32)
        return z - inc[:, :tk], inc[:, tk:]

    def key_span(hh, sub_lo, sub_hi, q_sub0, run):
        rows = pl.ds(sub_lo * tk, (sub_hi - sub_lo) * tk)
        kb = kn_ref[rows, head_lanes[hh]]
        vb = vb_ref[rows, head_lanes[hh]]
        z_all = lax.dot_general(qns[hh], kb, (((1,), (1,)), ((), ())),
                                preferred_element_type=F32)
        ws = [None] * (sub_hi - sub_lo)
        for s in range(sub_hi - 1, sub_lo - 1, -1):
            vis = None
            if s >= q_sub0:
                vis = (lax.broadcasted_iota(jnp.int32, (tq, tk), 1) + (s - q_sub0) * tk
                       < lax.broadcasted_iota(jnp.int32, (tq, tk), 0))
            pre, tot = sub_block(z_all[:, (s - sub_lo) * tk:(s - sub_lo + 1) * tk], vis)
            w = jnp.exp2(pre + run)
            if vis is not None:
                w = jnp.where(vis, w, 0.0)
            ws[s - sub_lo] = w.astype(BF16)
            run = run - tot
        return jnp.dot(jnp.concatenate(ws, axis=1), vb, preferred_element_type=F32), run

    def query_block(qi):
        nh = len(head_lanes)
        q_sub0 = qi * nsub
        runs = [jnp.zeros((tq, LANES), F32) for _ in range(nh)]
        accs = [jnp.zeros((tq, SB_HD), F32) for _ in range(nh)]

        def sweep(sub_hi, sub_stop, accs, runs):
            accs, runs = list(accs), list(runs)
            while sub_hi > sub_stop:
                sub_lo = max(sub_hi - nsub, sub_stop)
                for hh in range(nh):
                    pv, runs[hh] = key_span(hh, sub_lo, sub_hi, q_sub0, runs[hh])
                    accs[hh] = accs[hh] + pv
                sub_hi = sub_lo
            return accs, runs

        always_from = max(q_sub0 - SB_ALWAYS_SUBS, 0)
        accs, runs = sweep(q_sub0 + nsub, always_from, accs, runs)
        if always_from > 0:
            live = functools.reduce(jnp.maximum, [jnp.max(r) for r in runs]) >= SB_UNDERFLOW_LOG2
            rest = lambda carry: tuple(sum(sweep(always_from, 0, carry[:nh], carry[nh:]), []))
            carry = lax.cond(live, rest, lambda carry: carry, tuple(accs) + tuple(runs))
            accs = carry[:nh]
        for hh, hl in enumerate(head_lanes):
            o_ref[:, hl] = (accs[hh] * _silu(z_ref[:, hl])).astype(o_ref.dtype)

    for qi in range(nq):
        pl.when(i == qi)(functools.partial(query_block, qi))


def _stickbreak(p, q_norm_g, k_norm_g, bsz, seq):
    tq = min(SB_TQ, seq)
    nq = seq // tq
    ng = SB_HEADS // SB_HEADS_PER_STEP
    gw = SB_HEADS_PER_STEP * SB_HD
    vec = pl.BlockSpec((1, SB_HD), lambda b, h, i: (0, 0))
    return pl.pallas_call(
        functools.partial(_sb_kernel, nq=nq),
        grid=(bsz, ng, nq),
        in_specs=[pl.BlockSpec((tq, gw), lambda b, h, i: (b * nq + i, h)),
                  pl.BlockSpec((seq, gw), lambda b, h, i: (b, ng + h)),
                  pl.BlockSpec((seq, gw), lambda b, h, i: (b, 2 * ng + h)),
                  pl.BlockSpec((tq, gw), lambda b, h, i: (b * nq + i, 3 * ng + h)),
                  vec, vec],
        out_specs=pl.BlockSpec((tq, gw), lambda b, h, i: (b * nq + i, h)),
        out_shape=jax.ShapeDtypeStruct((bsz * seq, SB_WIDTH), BF16),
        scratch_shapes=[pltpu.VMEM((seq, gw), BF16), pltpu.VMEM((seq, gw), BF16)],
        compiler_params=_params("parallel", "parallel", "arbitrary"),
        name="stickbreak",
    )(p, p, p, p, q_norm_g.reshape(1, SB_HD), k_norm_g.reshape(1, SB_HD))


def _even_layer(h, xn, mkv, k_norm_g, j, w_in, conv_qkv, a_log, dt_bias, dn_norm_g, dw_w, dw_b,
                ln_g, ln_b, q_norm_m, w_out, next_norm_g, bsz, seq):
    n_main = w_in.shape[2] - 2 * DN_HEADS
    w_in_t = jnp.swapaxes(w_in, 1, 2)
    p_main = _proj(xn, w_in_t, j, n_main, transposed=True,
                   skip_from=4 * DN_WIDTH // 1024, skip=2 * DN_HEADS, tn=1024)
    ba = _proj(xn, w_in_t, j, LANES, col0=4 * DN_WIDTH // LANES, transposed=True, tn=LANES)
    gates, gates_t = _gates(ba, a_log, dt_bias)
    o_a = _deltanet(p_main, gates, gates_t, conv_qkv, dn_norm_g, bsz, seq)
    o_b = _conformer(p_main, 4, 5, 6, dw_w, dw_b, ln_g, ln_b, bsz, seq)
    o_m = _mem_attend(p_main, 7, 8, mkv, q_norm_m, k_norm_g, bsz, seq)
    return _out_proj([o_a, o_b, o_m], w_out, j, [0, 1, 2], h, next_norm_g)


def _odd_layer(h, xn, mkv, k_norm_g, j, w_in, q_norm_c, k_norm_c, q_norm_m, w_out, next_norm_g,
               bsz, seq):
    p = _proj(xn, w_in, j, w_in.shape[2])
    o_c = _stickbreak(p, q_norm_c, k_norm_c, bsz, seq)
    o_m = _mem_attend(p, 8, 9, mkv, q_norm_m, k_norm_g, bsz, seq)
    return _out_proj([o_c, o_m], w_out, j, [0, SB_WIDTH // MEM_WIDTH], h, next_norm_g)


def kernel(x, mem, mem_norm_g, w_mem_kv, mem_k_norm_g, ev_norm_g, ev_w_in, ev_conv_qkv, ev_a_log,
           ev_dt_bias, ev_dn_norm_g, ev_dw_w, ev_dw_b, ev_ln_g, ev_ln_b, ev_q_norm_m, ev_w_out,
           od_norm_g, od_w_in, od_q_norm_c, od_k_norm_c, od_q_norm_m, od_w_out):
    bsz, seq, d = x.shape
    mlen = mem.shape[1]
    depth = ev_norm_g.shape[0] + od_norm_g.shape[0]
    mkv = _proj(_rmsnorm_bf16(mem.reshape(bsz * mlen, d), mem_norm_g, tm=256),
                w_mem_kv[None], 0, w_mem_kv.shape[1])
    h = x.reshape(bsz * seq, d)
    layer_gain = lambda layer: (ev_norm_g if layer % 2 == 0 else od_norm_g)[layer // 2]
    xn = _rmsnorm_bf16(h, layer_gain(0))
    for layer in range(depth):
        j = layer // 2
        next_g = layer_gain(layer + 1) if layer + 1 < depth else None
        if layer % 2 == 0:
            h, xn = _even_layer(h, xn, mkv, mem_k_norm_g, j, ev_w_in, ev_conv_qkv[j], ev_a_log[j],
                                ev_dt_bias[j], ev_dn_norm_g[j], ev_dw_w[j], ev_dw_b[j], ev_ln_g[j],
                                ev_ln_b[j], ev_q_norm_m[j], ev_w_out, next_g, bsz, seq)
        else:
            h, xn = _odd_layer(h, xn, mkv, mem_k_norm_g, j, od_w_in, od_q_norm_c[j], od_k_norm_c[j],
                               od_q_norm_m[j], od_w_out, next_g, bsz, seq)
    return h.reshape(bsz, seq, d)
```
